```python
import math
import jax, jax.numpy as jnp
from jax import lax
import numpy as np

D_MODEL = 1024
BATCH = 8
SEQ = 2048
DEPTH = 1
DEC_BATCH = 128
DEC_SEQ = 4
PAST_LEN = 16384
PAGE_SIZE = 128

N_DN_HEADS = 8
DN_HEAD_DIM = 64
DN_WIDTH = N_DN_HEADS * DN_HEAD_DIM
CONV_CH = D_MODEL - DN_WIDTH
CONV_GROUPS = 8
CONV_GROUP_DIM = CONV_CH // CONV_GROUPS
SHORT_CONV = 4
DW_CONV = 31
CHUNK = 64
D_FF = 4 * D_MODEL
D_PLE = 256
EPS = 1e-6
IN_COLS = 4 * DN_WIDTH + 2 * N_DN_HEADS + 2 * CONV_CH

kernel_name = 'hybrid_gdn_conformer_decode_step'


def rmsnorm(x, gain):
    xf = x.astype(jnp.float32)
    y = xf * lax.rsqrt(jnp.mean(xf * xf, axis=-1, keepdims=True) + EPS)
    return (y * gain.astype(jnp.float32)).astype(x.dtype)


def l2norm(x):
    xf = x.astype(jnp.float32)
    return xf * lax.rsqrt(jnp.sum(xf * xf, axis=-1, keepdims=True) + EPS)


def causal_depthwise_conv(x_full, w):
    ch = x_full.shape[-1]
    return lax.conv_general_dilated(
        x_full, w[:, None, :].astype(x_full.dtype), window_strides=(1,), padding='VALID',
        dimension_numbers=('NWC', 'WIO', 'NWC'), feature_group_count=ch)


def gated_delta_rule(q, k, v, g, beta, s0):
    b, t, h, dk = q.shape
    dv = v.shape[-1]
    c = min(CHUNK, t)
    n = -(-t // c)
    pad = n * c - t
    f32 = jnp.float32

    def prep(a):
        a = a.astype(f32)
        a = jnp.pad(a, [(0, 0), (0, pad)] + [(0, 0)] * (a.ndim - 2))
        a = a.reshape((b, n, c) + a.shape[2:])
        return jnp.moveaxis(a, 3, 1)

    q, k, v, g, beta = prep(q), prep(k), prep(v), prep(g), prep(beta)
    q = q * (dk ** -0.5)
    gc = jnp.cumsum(g, axis=-1)
    k_beta = k * beta[..., None]
    v_beta = v * beta[..., None]
    idx = jnp.arange(c)
    causal = idx[:, None] >= idx[None, :]
    strict = idx[:, None] > idx[None, :]
    diff = gc[..., :, None] - gc[..., None, :]
    decay = jnp.exp(jnp.where(causal, diff, -jnp.inf))
    lmat = jnp.where(strict, jnp.einsum('bhncd,bhnmd->bhncm', k_beta, k) * decay, 0.0)
    eye = jnp.eye(c, dtype=f32)
    tmat = lax.linalg.triangular_solve(eye + lmat, jnp.broadcast_to(eye, lmat.shape),
                                       left_side=True, lower=True, unit_diagonal=True)
    u = jnp.einsum('bhncm,bhnmd->bhncd', tmat, v_beta)
    w = jnp.einsum('bhncm,bhnmd->bhncd', tmat, k_beta * jnp.exp(gc)[..., None])
    qk = jnp.einsum('bhncd,bhnmd->bhncm', q, k) * decay
    q_dec = q * jnp.exp(gc)[..., None]
    k_dec = k * jnp.exp(gc[..., -1:] - gc)[..., None]
    g_last = jnp.exp(gc[..., -1])

    def step(s, xs):
        u_c, w_c, qk_c, qd_c, kd_c, gl_c = xs
        v_new = u_c - jnp.einsum('bhcd,bhde->bhce', w_c, s)
        o_c = jnp.einsum('bhcd,bhde->bhce', qd_c, s) + jnp.einsum('bhcm,bhme->bhce', qk_c, v_new)
        s = s * gl_c[..., None, None] + jnp.einsum('bhcd,bhce->bhde', kd_c, v_new)
        return s, o_c

    xs = tuple(jnp.moveaxis(a, 2, 0) for a in (u, w, qk, q_dec, k_dec, g_last))
    s_final, o = lax.scan(step, s0.astype(f32), xs)
    o = jnp.transpose(o, (1, 0, 3, 2, 4)).reshape(b, n * c, h, dv)[:, :t]
    return o, s_final


def decoder_layer(x, p_emb, qkv_buf, conf_buf, s0, g_mix, w_in, w_short, a_log, dt_bias, g_o,
                  w_dw, b_dw, g_cln, b_cln, w_out, g_mlp, w_up, w_down, w_ple, g_ple, w_ple_gate):
    f32 = jnp.float32
    bsz, t, _ = x.shape
    h = rmsnorm(x, g_mix)
    proj = h @ w_in
    o0 = 3 * DN_WIDTH
    o1 = o0 + DN_WIDTH
    o2 = o1 + 2 * N_DN_HEADS
    qkv_in = proj[..., :o0]
    z = proj[..., o0:o1]
    b_logit = proj[..., o1:o1 + N_DN_HEADS]
    a_logit = proj[..., o1 + N_DN_HEADS:o2]
    glu_in = proj[..., o2:]

    qkv_full = jnp.concatenate([qkv_buf.astype(x.dtype), qkv_in], axis=1)
    new_qkv_buf = qkv_full[:, -(SHORT_CONV - 1):]
    qkv = jax.nn.silu(causal_depthwise_conv(qkv_full, w_short))
    heads = lambda a: a.reshape(bsz, t, N_DN_HEADS, DN_HEAD_DIM)
    q = l2norm(heads(qkv[..., :DN_WIDTH]))
    k = l2norm(heads(qkv[..., DN_WIDTH:2 * DN_WIDTH]))
    v = heads(qkv[..., 2 * DN_WIDTH:])
    beta = jax.nn.sigmoid(b_logit.astype(f32))
    g = -jnp.exp(a_log.astype(f32)) * jax.nn.softplus(a_logit.astype(f32) + dt_bias.astype(f32))
    o, s_new = gated_delta_rule(q, k, v, g, beta, s0)
    o = rmsnorm(o, g_o) * jax.nn.silu(heads(z).astype(f32))
    o = o.reshape(bsz, t, DN_WIDTH).astype(x.dtype)

    glu = glu_in[..., :CONV_CH] * jax.nn.sigmoid(glu_in[..., CONV_CH:])
    conf_full = jnp.concatenate([conf_buf.astype(x.dtype), glu], axis=1)
    new_conf_buf = conf_full[:, -(DW_CONV - 1):]
    cv = causal_depthwise_conv(conf_full, w_dw) + b_dw.astype(x.dtype)
    cf = cv.astype(f32).reshape(bsz, t, CONV_GROUPS, CONV_GROUP_DIM)
    mu = jnp.mean(cf, axis=-1, keepdims=True)
    var = jnp.mean(jnp.square(cf - mu), axis=-1, keepdims=True)
    cf = ((cf - mu) * lax.rsqrt(var + EPS)).reshape(bsz, t, CONV_CH) * g_cln.astype(f32) + b_cln.astype(f32)
    cv = jax.nn.silu(cf).astype(x.dtype)

    x = x + jnp.concatenate([o, cv], axis=-1) @ w_out

    h = rmsnorm(x, g_mlp)
    x = x + jnp.square(jax.nn.relu(h @ w_up)) @ w_down

    gate = jax.nn.sigmoid(rmsnorm(x, g_ple) @ w_ple_gate)
    x = x + (p_emb.astype(x.dtype) @ w_ple) * gate
    return x, new_qkv_buf, new_conf_buf, s_new.astype(s0.dtype)


def setup_inputs(seed: int = 0) -> dict:
    key = jax.random.key(seed)
    ks = jax.random.split(key, 32)
    f32 = jnp.float32
    nrm = lambda k, shape, scale: jax.random.normal(k, shape, f32) * scale
    gain = lambda k, shape: 1.0 + 0.05 * jax.random.normal(k, shape, f32)
    dt = jnp.exp(jax.random.uniform(ks[20], (DEPTH, N_DN_HEADS), f32, math.log(1e-3), math.log(0.1)))
    dt_bias = dt + jnp.log(-jnp.expm1(-dt))
    a_log = jnp.log(jax.random.uniform(ks[21], (DEPTH, N_DN_HEADS), f32, 1.0, 16.0))
    return {
        'x_prompt': nrm(ks[0], (BATCH, SEQ, D_MODEL), 1.0),
        'x_sample': nrm(ks[1], (DEC_BATCH, DEC_SEQ, D_MODEL), 1.0),
        'state_ssm': nrm(ks[2], (DEPTH, DEC_BATCH, N_DN_HEADS, DN_HEAD_DIM, DN_HEAD_DIM), 0.05),
        'state_qkv_conv': nrm(ks[3], (DEPTH, DEC_BATCH, SHORT_CONV - 1, 3 * DN_WIDTH), 1.0),
        'state_conf_conv': nrm(ks[4], (DEPTH, DEC_BATCH, DW_CONV - 1, CONV_CH), 0.5),
        'p_prompt': nrm(ks[5], (DEPTH, BATCH, SEQ, D_PLE), 1.0),
        'p_sample': nrm(ks[6], (DEPTH, DEC_BATCH, DEC_SEQ, D_PLE), 1.0),
        'g_mix': gain(ks[7], (DEPTH, D_MODEL)),
        'w_in': nrm(ks[8], (DEPTH, D_MODEL, IN_COLS), D_MODEL ** -0.5),
        'w_short': nrm(ks[9], (DEPTH, SHORT_CONV, 3 * DN_WIDTH), SHORT_CONV ** -0.5),
        'a_log': a_log,
        'dt_bias': dt_bias,
        'g_o': gain(ks[10], (DEPTH, DN_HEAD_DIM)),
        'w_dw': nrm(ks[11], (DEPTH, DW_CONV, CONV_CH), DW_CONV ** -0.5),
        'b_dw': nrm(ks[12], (DEPTH, CONV_CH), 0.02),
        'g_cln': gain(ks[13], (DEPTH, CONV_CH)),
        'b_cln': nrm(ks[14], (DEPTH, CONV_CH), 0.02),
        'w_out': nrm(ks[15], (DEPTH, D_MODEL, D_MODEL), D_MODEL ** -0.5),
        'g_mlp': gain(ks[16], (DEPTH, D_MODEL)),
        'w_up': nrm(ks[17], (DEPTH, D_MODEL, D_FF), D_MODEL ** -0.5),
        'w_down': nrm(ks[18], (DEPTH, D_FF, D_MODEL), D_FF ** -0.5),
        'w_ple': nrm(ks[19], (DEPTH, D_PLE, D_MODEL), D_PLE ** -0.5),
        'g_ple': gain(ks[22], (DEPTH, D_MODEL)),
        'w_ple_gate': nrm(ks[23], (DEPTH, D_MODEL, D_MODEL), D_MODEL ** -0.5),
        'g_final': gain(ks[24], (D_MODEL,)),
    }


def reference(x_prompt, x_sample, state_ssm, state_qkv_conv, state_conf_conv, p_prompt, p_sample,
              g_mix, w_in, w_short, a_log, dt_bias, g_o, w_dw, b_dw, g_cln, b_cln, w_out,
              g_mlp, w_up, w_down, w_ple, g_ple, w_ple_gate, g_final):
    bp = x_prompt.shape[0]
    yp, ys = x_prompt, x_sample
    ssm_p, qkv_p, conf_p, ssm_s, qkv_s, conf_s = [], [], [], [], [], []
    for i in range(DEPTH):
        params = (g_mix[i], w_in[i], w_short[i], a_log[i], dt_bias[i], g_o[i], w_dw[i], b_dw[i],
                  g_cln[i], b_cln[i], w_out[i], g_mlp[i], w_up[i], w_down[i], w_ple[i], g_ple[i],
                  w_ple_gate[i])
        zq = jnp.zeros((bp, SHORT_CONV - 1, 3 * DN_WIDTH), x_prompt.dtype)
        zc = jnp.zeros((bp, DW_CONV - 1, CONV_CH), x_prompt.dtype)
        zs = jnp.zeros((bp, N_DN_HEADS, DN_HEAD_DIM, DN_HEAD_DIM), state_ssm.dtype)
        yp, bq, bc, sp = decoder_layer(yp, p_prompt[i], zq, zc, zs, *params)
        ys, sq, sc, ss = decoder_layer(ys, p_sample[i], state_qkv_conv[i], state_conf_conv[i],
                                       state_ssm[i], *params)
        ssm_p.append(sp); qkv_p.append(bq); conf_p.append(bc)
        ssm_s.append(ss); qkv_s.append(sq); conf_s.append(sc)
    y_prompt = rmsnorm(yp, g_final)
    y_sample = rmsnorm(ys, g_final)
    return (y_prompt, y_sample, jnp.stack(ssm_p), jnp.stack(qkv_p), jnp.stack(conf_p),
            jnp.stack(ssm_s), jnp.stack(qkv_s), jnp.stack(conf_s))
```

```python
import functools

import jax
import jax.numpy as jnp
from jax import lax
from jax.experimental import pallas as pl
from jax.experimental.pallas import tpu as pltpu

F32 = jnp.float32
BF16 = jnp.bfloat16
EPS = 1e-6
HEAD_DIM = 64
CHUNK = 64
GROUP_W = 256
HEADS_PER_GROUP = GROUP_W // HEAD_DIM
NEG_BIG = -1e30
VMEM_LIMIT = 56 * 1024 * 1024


def _dot(a, b):
    return jnp.dot(a.astype(BF16), b.astype(BF16), preferred_element_type=F32)


def _dot_nt(a, b):
    return lax.dot_general(a.astype(BF16), b.astype(BF16), (((1,), (1,)), ((), ())), preferred_element_type=F32)


def _sigmoid(x):
    return 1.0 / (1.0 + jnp.exp(-x))


def _silu(x):
    return x * _sigmoid(x)


def _softplus(x):
    return jnp.maximum(x, 0.0) + jnp.log1p(jnp.exp(-jnp.abs(x)))


def _rms(x, gain):
    return x * lax.rsqrt(jnp.mean(x * x, axis=-1, keepdims=True) + EPS) * gain


def _split3(x):
    hi = x.astype(BF16)
    r = x - hi.astype(F32)
    mid = r.astype(BF16)
    lo = (r - mid.astype(F32)).astype(BF16)
    return hi, mid, lo


def _dot_exact(a01, x):
    hi, mid, lo = _split3(x)
    d = lambda p: jnp.dot(a01, p, preferred_element_type=F32)
    return d(hi) + (d(mid) + d(lo))


def _group_sum(x, ones_bd):
    rows = x.shape[0]
    outs = []
    for blk in range(x.shape[1] // GROUP_W):
        xs = x[:, GROUP_W * blk:GROUP_W * (blk + 1)]
        hi = xs.astype(BF16)
        lo = (xs - hi.astype(F32)).astype(BF16)
        r = jnp.dot(jnp.concatenate([hi, lo], axis=0), ones_bd, preferred_element_type=F32)
        outs.append(r[:rows] + r[rows:])
    return jnp.concatenate(outs, axis=1)


def _expand_heads(cols, off, n_heads):
    rows = cols.shape[0]
    lo_half = lax.broadcasted_iota(jnp.int32, (rows, 128), 1) < HEAD_DIM
    blocks = []
    for b in range(n_heads // 2):
        c0 = cols[:, off + 2 * b:off + 2 * b + 1]
        c1 = cols[:, off + 2 * b + 1:off + 2 * b + 2]
        blocks.append(jnp.where(lo_half, c0, c1))
    return jnp.concatenate(blocks, axis=1)


class _Masks:
    def __init__(self, seqs):
        c = CHUNK
        ri = lax.broadcasted_iota(jnp.int32, (c, GROUP_W), 0)
        lj = lax.broadcasted_iota(jnp.int32, (c, GROUP_W), 1) & (HEAD_DIM - 1)
        same = (ri & (seqs - 1)) == (lj & (seqs - 1))
        self.strict = same & (ri > lj)
        self.causal = same & (ri >= lj)
        self.eye_cat = (ri == lj).astype(F32)
        r2 = lax.broadcasted_iota(jnp.int32, (GROUP_W, GROUP_W), 0)
        c2 = lax.broadcasted_iota(jnp.int32, (GROUP_W, GROUP_W), 1)
        self.bd = (r2 >> 6) == (c2 >> 6)
        self.ones_bd = self.bd.astype(BF16)
        r3 = lax.broadcasted_iota(jnp.int32, (c, c), 0)
        c3 = lax.broadcasted_iota(jnp.int32, (c, c), 1)
        self.tri = (((r3 & (seqs - 1)) == (c3 & (seqs - 1))) & (r3 >= c3)).astype(BF16)
        self.ones_cc = jnp.ones((c, c), BF16)


def _block_diag(y, m):
    yb = y.astype(BF16)
    return jnp.where(m.bd, jnp.concatenate([yb] * HEADS_PER_GROUP, axis=0), jnp.zeros((), BF16))


def _intra_chunk(qs, k, v, bexp, g_row, g_last, m, levels):
    egc = jnp.exp(g_row)
    kb = k * bexp
    vb = v * bexp
    kbd = kb * egc
    qd = qs * egc
    kd = k * jnp.exp(g_last - g_row)
    outs = []
    for grp in range(qs.shape[1] // GROUP_W):
        sl = slice(GROUP_W * grp, GROUP_W * (grp + 1))
        gr = g_row[:, sl]
        g_col = _dot_exact(m.ones_cc, gr * m.eye_cat)
        dec = jnp.exp(jnp.where(m.causal, gr - g_col, NEG_BIG))
        k_bd = _block_diag(k[:, sl], m)
        aq = _dot_nt(jnp.concatenate([kb[:, sl], qs[:, sl]], axis=0), k_bd)
        a = jnp.where(m.strict, aq[:CHUNK] * dec, 0.0)
        qk = aq[CHUNK:] * dec
        pw = -a
        t = m.eye_cat + pw
        for _ in range(levels):
            pw = _dot(pw, _block_diag(pw, m))
            t = t + _dot(t, _block_diag(pw, m))
        uw = _dot(t, jnp.concatenate([_block_diag(vb[:, sl], m), _block_diag(kbd[:, sl], m)], axis=1))
        outs.append((uw[:, :GROUP_W], uw[:, GROUP_W:], qk, qd[:, sl], kd[:, sl]))
    return outs


def _qk_norm(x, ones_bd):
    return x * lax.rsqrt(_group_sum(x * x, ones_bd) + EPS)


def _gated_out_norm(o, z_silu, g_o, ones_bd):
    ms = _group_sum(o * o, ones_bd) * (1.0 / HEAD_DIM)
    return o * lax.rsqrt(ms + EPS) * g_o * z_silu


def _conf_post(cv, g_cln, b_cln, ones_bd):
    mu = _group_sum(cv, ones_bd) * (1.0 / HEAD_DIM)
    d = cv - mu
    var = _group_sum(d * d, ones_bd) * (1.0 / HEAD_DIM)
    return _silu(d * lax.rsqrt(var + EPS) * g_cln + b_cln)


def _inproj_kernel(n_heads, dn, cc, x_ref, gmix_ref, wmain_ref, wba_ref, gp_ref, qkv_ref, z_ref, ba_ref, glu_ref):
    hb = _rms(x_ref[...], gmix_ref[...]).astype(BF16)
    d = lambda w: jnp.dot(hb, w, preferred_element_type=F32)
    o_z = 3 * dn
    o_ga = o_z + dn
    o_gb = o_ga + cc
    qkv_ref[...] = d(wmain_ref[:, :o_z])
    z_ref[...] = _silu(d(wmain_ref[:, o_z:o_ga]))
    glu_ref[...] = d(wmain_ref[:, o_ga:o_gb]) * _sigmoid(d(wmain_ref[:, o_gb:o_gb + cc]))
    ba = d(wba_ref[...])
    col = lax.broadcasted_iota(jnp.int32, ba.shape, 1)
    g = -jnp.exp(gp_ref[0:1, :]) * _softplus(ba + gp_ref[1:2, :])
    ba_ref[...] = jnp.where(col < n_heads, _sigmoid(ba), g)


def _inproj(x, gmix, wmain, wba, gp, n_heads, dn, cc, tm):
    t, d = x.shape
    const = lambda shape: pl.BlockSpec(shape, lambda i: (0, 0), pipeline_mode=pl.Buffered(1))
    row = lambda w: pl.BlockSpec((tm, w), lambda i: (i, 0))
    return pl.pallas_call(
        functools.partial(_inproj_kernel, n_heads, dn, cc),
        grid=(t // tm,),
        in_specs=[row(d), const(gmix.shape), const(wmain.shape), const(wba.shape), const(gp.shape)],
        out_specs=[row(3 * dn), row(dn), row(128), row(cc)],
        out_shape=[jax.ShapeDtypeStruct((t, 3 * dn), F32), jax.ShapeDtypeStruct((t, dn), F32),
                   jax.ShapeDtypeStruct((t, 128), F32), jax.ShapeDtypeStruct((t, cc), F32)],
        compiler_params=pltpu.CompilerParams(dimension_semantics=("arbitrary",), vmem_limit_bytes=VMEM_LIMIT),
        name="inproj",
    )(x, gmix, wmain, wba, gp)


def _outmlp_kernel(ff_chunk, x_ref, ocv_ref, p_ref, wout_ref, gmlp_ref, wup_ref, wdown_ref, gple_ref, wgate_ref,
                   wple_ref, gfin_ref, y_ref):
    x = x_ref[...] + jnp.dot(ocv_ref[...], wout_ref[...], preferred_element_type=F32)
    hb = _rms(x, gmlp_ref[...]).astype(BF16)
    for f in range(wup_ref.shape[1] // ff_chunk):
        up = jnp.maximum(jnp.dot(hb, wup_ref[:, f * ff_chunk:(f + 1) * ff_chunk], preferred_element_type=F32), 0.0)
        x = x + jnp.dot((up * up).astype(BF16), wdown_ref[f * ff_chunk:(f + 1) * ff_chunk, :],
                        preferred_element_type=F32)
    gate = _sigmoid(_dot(_rms(x, gple_ref[...]), wgate_ref[...]))
    x = x + _dot(p_ref[...], wple_ref[...]) * gate
    y_ref[...] = _rms(x, gfin_ref[...])


def _outmlp(x, ocv, p, wout, gmlp, wup, wdown, gple, wgate, wple, gfin, tm):
    t, d = x.shape
    const = lambda a: pl.BlockSpec(a.shape, lambda i: (0, 0), pipeline_mode=pl.Buffered(1))
    row = lambda w: pl.BlockSpec((tm, w), lambda i: (i, 0))
    return pl.pallas_call(
        functools.partial(_outmlp_kernel, 1024),
        grid=(t // tm,),
        in_specs=[row(d), row(ocv.shape[1]), row(p.shape[1]), const(wout), const(gmlp), const(wup), const(wdown),
                  const(gple), const(wgate), const(wple), const(gfin)],
        out_specs=row(d),
        out_shape=jax.ShapeDtypeStruct((t, d), F32),
        compiler_params=pltpu.CompilerParams(dimension_semantics=("arbitrary",), vmem_limit_bytes=VMEM_LIMIT),
        name="outmlp",
    )(x, ocv, p, wout, gmlp, wup, wdown, gple, wgate, wple, gfin)


def _seq_prompt_kernel(tt, dn, n_short, n_dw, qkv_ref, z_ref, ba_ref, glu_ref, wshort_ref, go_ref, wdw_ref, bdw_ref,
                       gcln_ref, bcln_ref, ocv_ref, sout_ref, qh_ref, ch_ref, xbuf, cbuf, sbd):
    step = pl.program_id(1)
    xpad = xbuf.shape[0] - tt
    cpad = cbuf.shape[0] - tt
    n_heads = dn // HEAD_DIM

    @pl.when(step == 0)
    def _():
        xbuf[0:xpad] = jnp.zeros((xpad, xbuf.shape[1]), F32)
        cbuf[0:cpad] = jnp.zeros((cpad, cbuf.shape[1]), F32)
        sbd[...] = jnp.zeros(sbd.shape, F32)

    xbuf[xpad:xpad + tt] = qkv_ref[...]
    cbuf[cpad:cpad + tt] = glu_ref[...]
    m = _Masks(1)
    levels = (CHUNK - 1).bit_length() - 1

    for c in range(tt // CHUNK):
        r0 = CHUNK * c
        base = xpad - (n_short - 1) + r0
        acc = xbuf[base:base + CHUNK] * wshort_ref[0:1]
        for j in range(1, n_short):
            acc = acc + xbuf[base + j:base + j + CHUNK] * wshort_ref[j:j + 1]
        qkv = _silu(acc)
        q = _qk_norm(qkv[:, :dn], m.ones_bd) * (HEAD_DIM ** -0.5)
        k = _qk_norm(qkv[:, dn:2 * dn], m.ones_bd)
        v = qkv[:, 2 * dn:]
        ba = ba_ref[r0:r0 + CHUNK]
        gcs = _dot_exact(m.tri, ba)
        bexp = _expand_heads(ba, 0, n_heads)
        g_row = _expand_heads(gcs, n_heads, n_heads)
        g_last = g_row[CHUNK - 1:CHUNK]
        o_parts = []
        for grp, (u, w, qk, qd, kd) in enumerate(_intra_chunk(q, k, v, bexp, g_row, g_last, m, levels)):
            sl = slice(GROUP_W * grp, GROUP_W * (grp + 1))
            s = sbd[grp]
            wq = _dot(jnp.concatenate([w, qd], axis=0), s)
            v_new = u - wq[:CHUNK]
            o_parts.append(wq[CHUNK:] + _dot(qk, _block_diag(v_new, m)))
            upd = _dot(kd.T, v_new)
            sbd[grp] = s * jnp.exp(g_last[:, sl]) + jnp.where(m.bd, upd, 0.0)
        o = _gated_out_norm(jnp.concatenate(o_parts, axis=1), z_ref[r0:r0 + CHUNK], go_ref[...], m.ones_bd)
        cbase = cpad - (n_dw - 1) + r0
        cacc = cbuf[cbase:cbase + CHUNK] * wdw_ref[0:1]
        for j in range(1, n_dw):
            cacc = cacc + cbuf[cbase + j:cbase + j + CHUNK] * wdw_ref[j:j + 1]
        cv = _conf_post(cacc + bdw_ref[...], gcln_ref[...], bcln_ref[...], m.ones_bd)
        ocv_ref[r0:r0 + CHUNK] = jnp.concatenate([o, cv], axis=1).astype(ocv_ref.dtype)

    xbuf[0:xpad] = xbuf[tt:tt + xpad]
    cbuf[0:cpad] = cbuf[tt:tt + cpad]
    qh_ref[0] = xbuf[xpad - (n_short - 1):xpad]
    ch_ref[0] = cbuf[cpad - (n_dw - 1):cpad]

    @pl.when(step == pl.num_programs(1) - 1)
    def _():
        for h in range(n_heads):
            grp, hl = divmod(h, HEADS_PER_GROUP)
            sout_ref[0, h] = sbd[grp][HEAD_DIM * hl:HEAD_DIM * (hl + 1), HEAD_DIM * hl:HEAD_DIM * (hl + 1)]


def _seq_prompt(qkv, z, ba, glu, wshort, go, wdw, bdw, gcln, bcln, batch, tt):
    t, dn3 = qkv.shape
    dn = dn3 // 3
    cc = glu.shape[1]
    seq = t // batch
    nt = seq // tt
    n_heads = dn // HEAD_DIM
    n_short, n_dw = wshort.shape[0], wdw.shape[0]
    xpad = -(-(n_short - 1) // 8) * 8
    cpad = -(-(n_dw - 1) // 8) * 8
    const = lambda a: pl.BlockSpec(a.shape, lambda b, i: (0, 0))
    row = lambda w: pl.BlockSpec((tt, w), lambda b, i: (b * nt + i, 0))
    return pl.pallas_call(
        functools.partial(_seq_prompt_kernel, tt, dn, n_short, n_dw),
        grid=(batch, nt),
        in_specs=[row(dn3), row(dn), row(128), row(cc), const(wshort), const(go), const(wdw), const(bdw),
                  const(gcln), const(bcln)],
        out_specs=[row(dn + cc),
                   pl.BlockSpec((1, n_heads, HEAD_DIM, HEAD_DIM), lambda b, i: (b, 0, 0, 0)),
                   pl.BlockSpec((1, n_short - 1, dn3), lambda b, i: (b, 0, 0)),
                   pl.BlockSpec((1, n_dw - 1, cc), lambda b, i: (b, 0, 0))],
        out_shape=[jax.ShapeDtypeStruct((t, dn + cc), BF16),
                   jax.ShapeDtypeStruct((batch, n_heads, HEAD_DIM, HEAD_DIM), F32),
                   jax.ShapeDtypeStruct((batch, n_short - 1, dn3), F32),
                   jax.ShapeDtypeStruct((batch, n_dw - 1, cc), F32)],
        scratch_shapes=[pltpu.VMEM((xpad + tt, dn3), F32), pltpu.VMEM((cpad + tt, cc), F32),
                        pltpu.VMEM((dn // GROUP_W, GROUP_W, GROUP_W), F32)],
        compiler_params=pltpu.CompilerParams(dimension_semantics=("arbitrary", "arbitrary"),
                                             vmem_limit_bytes=VMEM_LIMIT),
        name="seq_prompt",
    )(qkv, z, ba, glu, wshort, go, wdw, bdw, gcln, bcln)


def _seq_sample_kernel(dn, hq_ref, xq_ref, z_ref, ba_ref, hc_ref, glu_ref, s0_ref, wshort_ref, go_ref, wdw_ref,
                       bdw_ref, gcln_ref, bcln_ref, ocv_ref, sout_ref, nq_ref, nc_ref):
    n_tok, seqs = xq_ref.shape[0], xq_ref.shape[1]
    n_heads = dn // HEAD_DIM
    n_short, n_dw = wshort_ref.shape[0], wdw_ref.shape[0]
    m = _Masks(seqs)
    levels = max((n_tok - 1).bit_length() - 1, 0)

    full_q = [hq_ref[j] for j in range(n_short - 1)] + [xq_ref[t] for t in range(n_tok)]
    for j in range(n_short - 1):
        nq_ref[j] = full_q[n_tok + j]
    rows = []
    for t in range(n_tok):
        acc = full_q[t] * wshort_ref[0:1]
        for j in range(1, n_short):
            acc = acc + full_q[t + j] * wshort_ref[j:j + 1]
        rows.append(acc)
    qkv = _silu(jnp.concatenate(rows, axis=0))
    q = _qk_norm(qkv[:, :dn], m.ones_bd) * (HEAD_DIM ** -0.5)
    k = _qk_norm(qkv[:, dn:2 * dn], m.ones_bd)
    v = qkv[:, 2 * dn:]
    ba = jnp.concatenate([ba_ref[t] for t in range(n_tok)], axis=0)
    gcs = _dot_exact(m.tri, ba)
    bexp = _expand_heads(ba, 0, n_heads)
    g_row = _expand_heads(gcs, n_heads, n_heads)
    g_end = g_row[(n_tok - 1) * seqs:]
    g_last = jnp.concatenate([g_end] * n_tok, axis=0)
    row_seq = lax.broadcasted_iota(jnp.int32, (2 * CHUNK, GROUP_W), 0) & (seqs - 1)
    col_seq = lax.broadcasted_iota(jnp.int32, (GROUP_W, CHUNK), 1) & (seqs - 1)
    o_parts = []
    for grp, (u, w, qk, qd, kd) in enumerate(_intra_chunk(q, k, v, bexp, g_row, g_last, m, levels)):
        sl = slice(GROUP_W * grp, GROUP_W * (grp + 1))
        lhs = jnp.concatenate([w, qd], axis=0).astype(BF16)
        s_bd = []
        wq = jnp.zeros((2 * CHUNK, GROUP_W), F32)
        for s in range(seqs):
            s_rows = s0_ref[s, HEADS_PER_GROUP * grp:HEADS_PER_GROUP * (grp + 1)].reshape(GROUP_W, HEAD_DIM)
            s_full = jnp.where(m.bd, jnp.concatenate([s_rows] * HEADS_PER_GROUP, axis=1), 0.0)
            s_bd.append(s_full)
            wq = jnp.where(row_seq == s, _dot(lhs, s_full), wq)
        v_new = u - wq[:CHUNK]
        o_parts.append(wq[CHUNK:] + _dot(qk, _block_diag(v_new, m)))
        kd_t = kd.T.astype(BF16)
        for s in range(seqs):
            upd = _dot(jnp.where(col_seq == s, kd_t, jnp.zeros((), BF16)), v_new)
            s_new = s_bd[s] * jnp.exp(g_end[s:s + 1, sl]) + jnp.where(m.bd, upd, 0.0)
            for hl in range(HEADS_PER_GROUP):
                sout_ref[s, HEADS_PER_GROUP * grp + hl] = s_new[HEAD_DIM * hl:HEAD_DIM * (hl + 1),
                                                                 HEAD_DIM * hl:HEAD_DIM * (hl + 1)]
    z = jnp.concatenate([z_ref[t] for t in range(n_tok)], axis=0)
    o = _gated_out_norm(jnp.concatenate(o_parts, axis=1), z, go_ref[...], m.ones_bd)

    full_c = [hc_ref[j] for j in range(n_dw - 1)] + [glu_ref[t] for t in range(n_tok)]
    for j in range(n_dw - 1):
        nc_ref[j] = full_c[n_tok + j]
    crows = []
    for t in range(n_tok):
        acc = full_c[t] * wdw_ref[0:1]
        for j in range(1, n_dw):
            acc = acc + full_c[t + j] * wdw_ref[j:j + 1]
        crows.append(acc)
    cv = _conf_post(jnp.concatenate(crows, axis=0) + bdw_ref[...], gcln_ref[...], bcln_ref[...], m.ones_bd)
    ocv = jnp.concatenate([o, cv], axis=1).astype(ocv_ref.dtype)
    for t in range(n_tok):
        ocv_ref[t] = ocv[t * seqs:(t + 1) * seqs]


def _seq_sample(hq, xq, z, ba, hc, glu, s0, wshort, go, wdw, bdw, gcln, bcln):
    n_tok, nseq, dn3 = xq.shape
    dn = dn3 // 3
    cc = glu.shape[2]
    seqs = CHUNK // n_tok
    n_heads = dn // HEAD_DIM
    const = lambda a: pl.BlockSpec(a.shape, lambda i: (0, 0))
    tm = lambda a: pl.BlockSpec((a.shape[0], seqs, a.shape[2]), lambda i: (0, i, 0))
    st = pl.BlockSpec((seqs, n_heads, HEAD_DIM, HEAD_DIM), lambda i: (i, 0, 0, 0))
    ocv_shape = jax.ShapeDtypeStruct((n_tok, nseq, dn + cc), BF16)
    return pl.pallas_call(
        functools.partial(_seq_sample_kernel, dn),
        grid=(nseq // seqs,),
        in_specs=[tm(hq), tm(xq), tm(z), tm(ba), tm(hc), tm(glu), st, const(wshort), const(go), const(wdw),
                  const(bdw), const(gcln), const(bcln)],
        out_specs=[tm(ocv_shape), st, tm(hq), tm(hc)],
        out_shape=[ocv_shape, jax.ShapeDtypeStruct(s0.shape, F32), jax.ShapeDtypeStruct(hq.shape, F32),
                   jax.ShapeDtypeStruct(hc.shape, F32)],
        compiler_params=pltpu.CompilerParams(dimension_semantics=("arbitrary",), vmem_limit_bytes=VMEM_LIMIT),
        name="seq_sample",
    )(hq, xq, z, ba, hc, glu, s0, wshort, go, wdw, bdw, gcln, bcln)


def kernel(x_prompt, x_sample, state_ssm, state_qkv_conv, state_conf_conv, p_prompt, p_sample, g_mix, w_in, w_short,
           a_log, dt_bias, g_o, w_dw, b_dw, g_cln, b_cln, w_out, g_mlp, w_up, w_down, w_ple, g_ple, w_ple_gate,
           g_final):
    depth = w_in.shape[0]
    assert depth == 1, "single-layer trunk"
    bp, seq, d = x_prompt.shape
    bs, n_tok, _ = x_sample.shape
    n_heads = a_log.shape[1]
    dn = n_heads * HEAD_DIM
    cc = w_dw.shape[2]
    o_ba = 4 * dn
    o_glu = o_ba + 2 * n_heads
    assert dn % GROUP_W == 0 and cc % GROUP_W == 0 and 2 * n_heads <= 128 and CHUNK % n_tok == 0

    wi = w_in[0]
    wmain = jnp.concatenate([wi[:, :o_ba], wi[:, o_glu:]], axis=1).astype(BF16)
    wba = jnp.pad(wi[:, o_ba:o_glu], ((0, 0), (0, 128 - 2 * n_heads))).astype(BF16)
    gp = jnp.zeros((2, 128), F32).at[0, n_heads:2 * n_heads].set(a_log[0]).at[1, n_heads:2 * n_heads].set(dt_bias[0])
    row = lambda a: a.reshape(1, -1)
    go = row(jnp.tile(g_o[0], n_heads))
    wout, wup, wdown = w_out[0].astype(BF16), w_up[0].astype(BF16), w_down[0].astype(BF16)
    wgate, wple = w_ple_gate[0].astype(BF16), w_ple[0].astype(BF16)
    small = (w_short[0], go, w_dw[0], row(b_dw[0]), row(g_cln[0]), row(b_cln[0]))
    tail = (wout, row(g_mlp[0]), wup, wdown, row(g_ple[0]), wgate, wple, row(g_final))

    xp = x_prompt.reshape(bp * seq, d)
    qkv, z, ba, glu = _inproj(xp, row(g_mix[0]), wmain, wba, gp, n_heads, dn, cc, tm=512)
    ocv, ssm_p, qkv_p, conf_p = _seq_prompt(qkv, z, ba, glu, *small, batch=bp, tt=128)
    y_prompt = _outmlp(xp, ocv, p_prompt[0].reshape(bp * seq, -1), *tail, tm=512).reshape(bp, seq, d)

    xs = jnp.swapaxes(x_sample, 0, 1).reshape(n_tok * bs, d)
    qkv, z, ba, glu = _inproj(xs, row(g_mix[0]), wmain, wba, gp, n_heads, dn, cc, tm=n_tok * bs)
    tmaj = lambda a: a.reshape(n_tok, bs, -1)
    ocv, ssm_s, qkv_s, conf_s = _seq_sample(
        jnp.swapaxes(state_qkv_conv[0], 0, 1), tmaj(qkv), tmaj(z), tmaj(ba),
        jnp.swapaxes(state_conf_conv[0], 0, 1), tmaj(glu), state_ssm[0], *small)
    ps = jnp.swapaxes(p_sample[0], 0, 1).reshape(n_tok * bs, -1)
    ys = _outmlp(xs, ocv.reshape(n_tok * bs, -1), ps, *tail, tm=n_tok * bs)
    y_sample = jnp.swapaxes(ys.reshape(n_tok, bs, d), 0, 1)

    return (y_prompt, y_sample, ssm_p[None], qkv_p[None], conf_p[None], ssm_s[None],
            jnp.swapaxes(qkv_s, 0, 1)[None], jnp.swapaxes(conf_s, 0, 1)[None])
```

```python
import functools

import jax
import jax.numpy as jnp
from jax import lax
from jax.experimental import pallas as pl
from jax.experimental.pallas import tpu as pltpu

F32 = jnp.float32
BF16 = jnp.bfloat16
EPS = 1e-6
HEAD_DIM = 64
CHUNK = 64
GROUP_W = 256
HEADS_PER_GROUP = GROUP_W // HEAD_DIM
SUBLANES = 8
VMEM_LIMIT = 56 * 1024 * 1024


def _dot(a, b):
    return jnp.dot(a.astype(BF16), b.astype(BF16), preferred_element_type=F32)


def _dot_nt(a, b):
    return lax.dot_general(a.astype(BF16), b.astype(BF16), (((1,), (1,)), ((), ())), preferred_element_type=F32)


def _sigmoid(x):
    return 1.0 / (1.0 + jnp.exp(-x))


def _silu(x):
    return x * _sigmoid(x)


def _softplus(x):
    return jnp.maximum(x, 0.0) + jnp.log1p(jnp.exp(-jnp.abs(x)))


def _rms(x, gain):
    return x * lax.rsqrt(jnp.mean(x * x, axis=-1, keepdims=True) + EPS) * gain


def _split3(x):
    hi = x.astype(BF16)
    r = x - hi.astype(F32)
    mid = r.astype(BF16)
    lo = (r - mid.astype(F32)).astype(BF16)
    return hi, mid, lo


def _dot_exact(a01, x):
    hi, mid, lo = _split3(x)
    d = lambda p: jnp.dot(a01, p, preferred_element_type=F32)
    return d(hi) + (d(mid) + d(lo))


def _block_ones(n):
    shift = HEAD_DIM.bit_length() - 1
    r = lax.broadcasted_iota(jnp.int32, (n, n), 0) >> shift
    c = lax.broadcasted_iota(jnp.int32, (n, n), 1) >> shift
    return (r == c).astype(BF16)


def _group_sum(x, ones_bd):
    outs = [jnp.dot(x[:, GROUP_W * b:GROUP_W * (b + 1)].astype(BF16), ones_bd, preferred_element_type=F32)
            for b in range(x.shape[1] // GROUP_W)]
    return jnp.concatenate(outs, axis=1)


def _expand_heads(cols, off, n_heads):
    rows = cols.shape[0]
    lo_half = lax.broadcasted_iota(jnp.int32, (rows, 128), 1) < HEAD_DIM
    blocks = []
    for b in range(n_heads // 2):
        c0 = cols[:, off + 2 * b:off + 2 * b + 1]
        c1 = cols[:, off + 2 * b + 1:off + 2 * b + 2]
        blocks.append(jnp.where(lo_half, c0, c1))
    return jnp.concatenate(blocks, axis=1)


class _Masks:
    def __init__(self, seqs):
        c = CHUNK
        ri = lax.broadcasted_iota(jnp.int32, (c, GROUP_W), 0)
        lj = lax.broadcasted_iota(jnp.int32, (c, GROUP_W), 1) & (HEAD_DIM - 1)
        same = (ri & (seqs - 1)) == (lj & (seqs - 1))
        self.strict = (same & (ri > lj)).astype(F32)
        self.causal = (same & (ri >= lj)).astype(F32)
        self.eye_cat = (ri == lj).astype(F32)
        self.bd = _block_ones(GROUP_W)
        self.bd_f32 = self.bd.astype(F32)
        r3 = lax.broadcasted_iota(jnp.int32, (c, c), 0)
        c3 = lax.broadcasted_iota(jnp.int32, (c, c), 1)
        self.tri = (((r3 & (seqs - 1)) == (c3 & (seqs - 1))) & (r3 >= c3)).astype(BF16)
        self.ones_cc = jnp.ones((c, c), BF16)


def _block_diag(y, m):
    return jnp.concatenate([y.astype(BF16)] * HEADS_PER_GROUP, axis=0) * m.bd


def _intra_prep(qs, k, v, bexp, g_row, g_last, m):
    egc = jnp.exp(g_row)
    kb = k * bexp
    vb = v * bexp
    kbd = kb * egc
    qd = qs * egc
    kd = k * jnp.exp(g_last - g_row)
    probs = []
    for grp in range(qs.shape[1] // GROUP_W):
        sl = slice(GROUP_W * grp, GROUP_W * (grp + 1))
        gr = g_row[:, sl]
        g_col = _dot_exact(m.ones_cc, gr * m.eye_cat)
        dec = jnp.exp(jnp.minimum(gr - g_col, 0.0)) * m.causal
        aq = _dot_nt(jnp.concatenate([kb[:, sl], qs[:, sl]], axis=0), _block_diag(k[:, sl], m))
        probs.append(dict(a=aq[:CHUNK] * (dec * m.strict), qk=aq[CHUNK:] * dec,
                          rhs=jnp.concatenate([_block_diag(vb[:, sl], m), _block_diag(kbd[:, sl], m)], axis=1),
                          qd=qd[:, sl], kd=kd[:, sl]))
    return probs


def _solve_uw(probs, m, levels):
    pws = [-p["a"] for p in probs]
    ts = [m.eye_cat + pw for pw in pws]
    for _ in range(levels):
        pws = [_dot(pw, _block_diag(pw, m)) for pw in pws]
        ts = [t + _dot(t, _block_diag(pw, m)) for t, pw in zip(ts, pws)]
    for p, t in zip(probs, ts):
        uw = _dot(t, p["rhs"])
        p["u"], p["w"] = uw[:, :GROUP_W], uw[:, GROUP_W:]


def _qkv_prep(acc, ba, dn, n_heads, m):
    qkv = _silu(acc)
    q, k, v = qkv[:, :dn], qkv[:, dn:2 * dn], qkv[:, 2 * dn:]
    q = q * (lax.rsqrt(_group_sum(q * q, m.bd) + EPS) * (HEAD_DIM ** -0.5))
    k = k * lax.rsqrt(_group_sum(k * k, m.bd) + EPS)
    gcs = _dot_exact(m.tri, ba)
    bexp = _expand_heads(ba, 0, n_heads)
    g_row = _expand_heads(gcs, n_heads, n_heads)
    return q, k, v, bexp, g_row


def _inproj_kernel(n_heads, dn, cc, x_ref, gmix_ref, wmain_ref, wba_ref, gp_ref, qkv_ref, z_ref, ba_ref, glu_ref):
    hb = _rms(x_ref[...], gmix_ref[...]).astype(BF16)
    d = lambda w: jnp.dot(hb, w, preferred_element_type=F32)
    o_z = 3 * dn
    o_ga = o_z + dn
    o_gb = o_ga + cc
    qkv_ref[...] = d(wmain_ref[:, :o_z])
    z_ref[...] = _silu(d(wmain_ref[:, o_z:o_ga]))
    glu_ref[...] = d(wmain_ref[:, o_ga:o_gb]) * _sigmoid(d(wmain_ref[:, o_gb:o_gb + cc]))
    ba = d(wba_ref[...])
    col = lax.broadcasted_iota(jnp.int32, ba.shape, 1)
    g = -jnp.exp(gp_ref[0:1, :]) * _softplus(ba + gp_ref[1:2, :])
    ba_ref[...] = jnp.where(col < n_heads, _sigmoid(ba), g)


def _inproj(x, gmix, wmain, wba, gp, n_heads, dn, cc, tm):
    t, d = x.shape
    const = lambda shape: pl.BlockSpec(shape, lambda i: (0, 0), pipeline_mode=pl.Buffered(1))
    row = lambda w: pl.BlockSpec((tm, w), lambda i: (i, 0))
    return pl.pallas_call(
        functools.partial(_inproj_kernel, n_heads, dn, cc),
        grid=(t // tm,),
        in_specs=[row(d), const(gmix.shape), const(wmain.shape), const(wba.shape), const(gp.shape)],
        out_specs=[row(3 * dn), row(dn), row(128), row(cc)],
        out_shape=[jax.ShapeDtypeStruct((t, 3 * dn), F32), jax.ShapeDtypeStruct((t, dn), F32),
                   jax.ShapeDtypeStruct((t, 128), F32), jax.ShapeDtypeStruct((t, cc), F32)],
        compiler_params=pltpu.CompilerParams(dimension_semantics=("arbitrary",), vmem_limit_bytes=VMEM_LIMIT),
        name="inproj",
    )(x, gmix, wmain, wba, gp)


def _seq_prompt_kernel(tt, dn, n_short, qkv_ref, ba_ref, wshort_ref, o_ref, sout_ref, qh_ref, xbuf, sbd):
    step = pl.program_id(1)
    xpad = xbuf.shape[0] - tt
    n_heads = dn // HEAD_DIM

    @pl.when(step == 0)
    def _():
        xbuf[0:xpad] = jnp.zeros((xpad, xbuf.shape[1]), F32)
        sbd[...] = jnp.zeros(sbd.shape, F32)

    xbuf[xpad:xpad + tt] = qkv_ref[...]
    m = _Masks(1)
    levels = (CHUNK - 1).bit_length() - 1

    chunks = []
    for c in range(tt // CHUNK):
        r0 = CHUNK * c
        base = xpad - (n_short - 1) + r0
        acc = xbuf[base:base + CHUNK] * wshort_ref[0:1]
        for j in range(1, n_short):
            acc = acc + xbuf[base + j:base + j + CHUNK] * wshort_ref[j:j + 1]
        q, k, v, bexp, g_row = _qkv_prep(acc, ba_ref[r0:r0 + CHUNK], dn, n_heads, m)
        g_last = g_row[CHUNK - 1:CHUNK]
        chunks.append((_intra_prep(q, k, v, bexp, g_row, g_last, m), g_last))
    _solve_uw([p for probs, _ in chunks for p in probs], m, levels)

    for c, (probs, g_last) in enumerate(chunks):
        o_parts = []
        for grp, p in enumerate(probs):
            sl = slice(GROUP_W * grp, GROUP_W * (grp + 1))
            s = sbd[grp]
            wq = _dot(jnp.concatenate([p["w"], p["qd"]], axis=0), s)
            v_new = p["u"] - wq[:CHUNK]
            o_parts.append(wq[CHUNK:] + _dot(p["qk"], _block_diag(v_new, m)))
            sbd[grp] = s * jnp.exp(g_last[:, sl]) + _dot(p["kd"].T, v_new) * m.bd_f32
        o_ref[CHUNK * c:CHUNK * (c + 1)] = jnp.concatenate(o_parts, axis=1)

    xbuf[0:xpad] = xbuf[tt:tt + xpad]
    qh_ref[0] = xbuf[xpad - (n_short - 1):xpad]

    @pl.when(step == pl.num_programs(1) - 1)
    def _():
        for h in range(n_heads):
            grp, hl = divmod(h, HEADS_PER_GROUP)
            sout_ref[0, h] = sbd[grp, HEAD_DIM * hl:HEAD_DIM * (hl + 1), HEAD_DIM * hl:HEAD_DIM * (hl + 1)]


def _seq_prompt(qkv, ba, wshort, batch, tt):
    t, dn3 = qkv.shape
    dn = dn3 // 3
    nt = t // batch // tt
    n_heads = dn // HEAD_DIM
    n_short = wshort.shape[0]
    xpad = -(-(n_short - 1) // SUBLANES) * SUBLANES
    row = lambda w: pl.BlockSpec((tt, w), lambda b, i: (b * nt + i, 0))
    return pl.pallas_call(
        functools.partial(_seq_prompt_kernel, tt, dn, n_short),
        grid=(batch, nt),
        in_specs=[row(dn3), row(128), pl.BlockSpec(wshort.shape, lambda b, i: (0, 0))],
        out_specs=[row(dn),
                   pl.BlockSpec((1, n_heads, HEAD_DIM, HEAD_DIM), lambda b, i: (b, 0, 0, 0)),
                   pl.BlockSpec((1, n_short - 1, dn3), lambda b, i: (b, 0, 0))],
        out_shape=[jax.ShapeDtypeStruct((t, dn), F32),
                   jax.ShapeDtypeStruct((batch, n_heads, HEAD_DIM, HEAD_DIM), F32),
                   jax.ShapeDtypeStruct((batch, n_short - 1, dn3), F32)],
        scratch_shapes=[pltpu.VMEM((xpad + tt, dn3), F32), pltpu.VMEM((dn // GROUP_W, GROUP_W, GROUP_W), F32)],
        compiler_params=pltpu.CompilerParams(dimension_semantics=("arbitrary", "arbitrary"),
                                             vmem_limit_bytes=VMEM_LIMIT),
        name="seq_prompt",
    )(qkv, ba, wshort)


def _seq_sample_kernel(dn, hq_ref, xq_ref, ba_ref, s0_ref, wshort_ref, o_ref, sout_ref, nq_ref):
    n_tok, seqs = xq_ref.shape[0], xq_ref.shape[1]
    n_heads = dn // HEAD_DIM
    n_short = wshort_ref.shape[0]
    m = _Masks(seqs)
    levels = max((n_tok - 1).bit_length() - 1, 0)

    full_q = [hq_ref[j] for j in range(n_short - 1)] + [xq_ref[t] for t in range(n_tok)]
    for j in range(n_short - 1):
        nq_ref[j] = full_q[n_tok + j]
    rows = []
    for t in range(n_tok):
        acc = full_q[t] * wshort_ref[0:1]
        for j in range(1, n_short):
            acc = acc + full_q[t + j] * wshort_ref[j:j + 1]
        rows.append(acc)
    ba = jnp.concatenate([ba_ref[t] for t in range(n_tok)], axis=0)
    q, k, v, bexp, g_row = _qkv_prep(jnp.concatenate(rows, axis=0), ba, dn, n_heads, m)
    g_end = g_row[(n_tok - 1) * seqs:]
    g_last = jnp.concatenate([g_end] * n_tok, axis=0)
    probs = _intra_prep(q, k, v, bexp, g_row, g_last, m)
    _solve_uw(probs, m, levels)

    row_seq = lax.broadcasted_iota(jnp.int32, (2 * CHUNK, GROUP_W), 0) & (seqs - 1)
    col_seq = lax.broadcasted_iota(jnp.int32, (GROUP_W, CHUNK), 1) & (seqs - 1)
    o_parts = []
    for grp, p in enumerate(probs):
        sl = slice(GROUP_W * grp, GROUP_W * (grp + 1))
        lhs = jnp.concatenate([p["w"], p["qd"]], axis=0).astype(BF16)
        s_bd = []
        wq = jnp.zeros((2 * CHUNK, GROUP_W), F32)
        for s in range(seqs):
            s_rows = s0_ref[s, HEADS_PER_GROUP * grp:HEADS_PER_GROUP * (grp + 1)].reshape(GROUP_W, HEAD_DIM)
            s_full = jnp.concatenate([s_rows] * HEADS_PER_GROUP, axis=1) * m.bd_f32
            s_bd.append(s_full)
            wq = jnp.where(row_seq == s, _dot(lhs, s_full), wq)
        v_new = p["u"] - wq[:CHUNK]
        o_parts.append(wq[CHUNK:] + _dot(p["qk"], _block_diag(v_new, m)))
        kd_t = p["kd"].T.astype(BF16)
        for s in range(seqs):
            upd = _dot(jnp.where(col_seq == s, kd_t, jnp.zeros((), BF16)), v_new)
            s_new = s_bd[s] * jnp.exp(g_end[s:s + 1, sl]) + upd * m.bd_f32
            for hl in range(HEADS_PER_GROUP):
                sout_ref[s, HEADS_PER_GROUP * grp + hl] = s_new[HEAD_DIM * hl:HEAD_DIM * (hl + 1),
                                                                 HEAD_DIM * hl:HEAD_DIM * (hl + 1)]
    o = jnp.concatenate(o_parts, axis=1)
    for t in range(n_tok):
        o_ref[t] = o[t * seqs:(t + 1) * seqs]


def _seq_sample(hq, xq, ba, s0, wshort):
    n_tok, nseq, dn3 = xq.shape
    dn = dn3 // 3
    seqs = CHUNK // n_tok
    n_heads = dn // HEAD_DIM
    tm = lambda a: pl.BlockSpec((a.shape[0], seqs, a.shape[2]), lambda i: (0, i, 0))
    st = pl.BlockSpec((seqs, n_heads, HEAD_DIM, HEAD_DIM), lambda i: (i, 0, 0, 0))
    o_shape = jax.ShapeDtypeStruct((n_tok, nseq, dn), F32)
    return pl.pallas_call(
        functools.partial(_seq_sample_kernel, dn),
        grid=(nseq // seqs,),
        in_specs=[tm(hq), tm(xq), tm(ba), st, pl.BlockSpec(wshort.shape, lambda i: (0, 0))],
        out_specs=[tm(o_shape), st, tm(hq)],
        out_shape=[o_shape, jax.ShapeDtypeStruct(s0.shape, F32), jax.ShapeDtypeStruct(hq.shape, F32)],
        compiler_params=pltpu.CompilerParams(dimension_semantics=("arbitrary",), vmem_limit_bytes=VMEM_LIMIT),
        name="seq_sample",
    )(hq, xq, ba, s0, wshort)


def _tail(x, o_raw, z_silu, conv, p, go_ref, bdw_ref, gcln_ref, bcln_ref, wout_ref, gmlp_ref, wup_ref, wdown_ref,
          gple_ref, wgate_ref, wple_ref, gfin_ref, ff_chunk):
    ones_bd = _block_ones(GROUP_W)
    inv = 1.0 / HEAD_DIM
    dn = o_raw.shape[1]
    o = o_raw * lax.rsqrt(_group_sum(o_raw * o_raw, ones_bd) * inv + EPS) * go_ref[...] * z_silu
    cv = conv + bdw_ref[...]
    d = cv - _group_sum(cv, ones_bd) * inv
    cv = _silu(d * lax.rsqrt(_group_sum(d * d, ones_bd) * inv + EPS) * gcln_ref[...] + bcln_ref[...])
    x = x + (_dot(o, wout_ref[:dn]) + _dot(cv, wout_ref[dn:]))
    hb = _rms(x, gmlp_ref[...]).astype(BF16)
    mlp = None
    for f in range(wup_ref.shape[1] // ff_chunk):
        up = jnp.maximum(jnp.dot(hb, wup_ref[:, f * ff_chunk:(f + 1) * ff_chunk], preferred_element_type=F32), 0.0)
        down = jnp.dot((up * up).astype(BF16), wdown_ref[f * ff_chunk:(f + 1) * ff_chunk, :],
                       preferred_element_type=F32)
        mlp = down if mlp is None else mlp + down
    x = x + mlp
    gate = _sigmoid(_dot(_rms(x, gple_ref[...]), wgate_ref[...]))
    x = x + _dot(p, wple_ref[...]) * gate
    return _rms(x, gfin_ref[...])


def _out_prompt_kernel(tm, tiles_per_seq, ff_chunk, x_ref, o_ref, z_ref, glu_ref, p_ref, wdw_ref, *rest):
    *tail_refs, y_ref, nc_ref, cbuf = rest
    n_dw = wdw_ref.shape[0]
    cpad = cbuf.shape[0] - tm - SUBLANES

    @pl.when(pl.program_id(0) % tiles_per_seq == 0)
    def _():
        cbuf[0:cpad] = jnp.zeros((cpad, cbuf.shape[1]), F32)
        cbuf[cpad + tm:] = jnp.zeros((SUBLANES, cbuf.shape[1]), F32)

    cbuf[cpad:cpad + tm] = glu_ref[...]
    first = cpad - (n_dw - 1)
    conv = None
    for s in range(SUBLANES):
        part = None
        for a in range((first + n_dw - 1) // SUBLANES + 1):
            j = SUBLANES * a + s - first
            if 0 <= j < n_dw:
                term = cbuf[SUBLANES * a:SUBLANES * a + tm + SUBLANES] * wdw_ref[j:j + 1]
                part = term if part is None else part + term
        if part is not None:
            conv = part[s:s + tm] if conv is None else conv + part[s:s + tm]
    y_ref[...] = _tail(x_ref[...], o_ref[...], z_ref[...], conv, p_ref[...], *tail_refs, ff_chunk)
    cbuf[0:cpad] = cbuf[tm:tm + cpad]
    nc_ref[0] = cbuf[cpad - (n_dw - 1):cpad]


def _out_sample_kernel(ff_chunk, x_ref, o_ref, z_ref, glu_ref, p_ref, wdw_ref, hc_ref, *rest):
    *tail_refs, y_ref, nc_ref = rest
    n_tok = glu_ref.shape[0]
    n_dw = wdw_ref.shape[0]
    full_c = [hc_ref[j] for j in range(n_dw - 1)] + [glu_ref[t] for t in range(n_tok)]
    for j in range(n_dw - 1):
        nc_ref[j] = full_c[n_tok + j]
    rows = []
    for t in range(n_tok):
        acc = full_c[t] * wdw_ref[0:1]
        for j in range(1, n_dw):
            acc = acc + full_c[t + j] * wdw_ref[j:j + 1]
        rows.append(acc)
    cat = lambda ref: jnp.concatenate([ref[t] for t in range(n_tok)], axis=0)
    y_ref[...] = _tail(x_ref[...], cat(o_ref), cat(z_ref), jnp.concatenate(rows, axis=0), p_ref[...], *tail_refs,
                       ff_chunk)


def _out_call(kernel_fn, name, x, tiled, consts, tm, extra_out_shape, extra_out_spec, scratch):
    t, d = x.shape
    const = lambda a: pl.BlockSpec(a.shape, lambda i: (0,) * a.ndim, pipeline_mode=pl.Buffered(1))
    row = lambda a: pl.BlockSpec((tm, a.shape[1]), lambda i: (i, 0))
    return pl.pallas_call(
        kernel_fn,
        grid=(t // tm,),
        in_specs=[row(x)] + [row(a) if a.ndim == 2 else pl.BlockSpec(a.shape, lambda i: (0, 0, 0)) for a in tiled]
        + [const(a) for a in consts],
        out_specs=[row(x), extra_out_spec],
        out_shape=[jax.ShapeDtypeStruct((t, d), F32), extra_out_shape],
        scratch_shapes=scratch,
        compiler_params=pltpu.CompilerParams(dimension_semantics=("arbitrary",), vmem_limit_bytes=VMEM_LIMIT),
        name=name,
    )(x, *tiled, *consts)


def kernel(x_prompt, x_sample, state_ssm, state_qkv_conv, state_conf_conv, p_prompt, p_sample, g_mix, w_in, w_short,
           a_log, dt_bias, g_o, w_dw, b_dw, g_cln, b_cln, w_out, g_mlp, w_up, w_down, w_ple, g_ple, w_ple_gate,
           g_final):
    depth = w_in.shape[0]
    assert depth == 1, "single-layer trunk"
    bp, seq, d = x_prompt.shape
    bs, n_tok, _ = x_sample.shape
    n_heads = a_log.shape[1]
    dn = n_heads * HEAD_DIM
    n_dw, cc = w_dw.shape[1], w_dw.shape[2]
    o_ba = 4 * dn
    o_glu = o_ba + 2 * n_heads
    assert dn % GROUP_W == 0 and cc % GROUP_W == 0 and 2 * n_heads <= 128 and CHUNK % n_tok == 0
    tm_in, tt, tm_out, ff_chunk = 512, 256, 512, 1024
    assert seq % tm_out == 0 and seq % tt == 0 and tt % CHUNK == 0 and (bp * seq) % tm_in == 0

    wi = w_in[0]
    wmain = jnp.concatenate([wi[:, :o_ba], wi[:, o_glu:]], axis=1).astype(BF16)
    wba = jnp.pad(wi[:, o_ba:o_glu], ((0, 0), (0, 128 - 2 * n_heads))).astype(BF16)
    gp = jnp.zeros((2, 128), F32).at[0, n_heads:2 * n_heads].set(a_log[0]).at[1, n_heads:2 * n_heads].set(dt_bias[0])
    row = lambda a: a.reshape(1, -1)
    tail = (row(jnp.tile(g_o[0], n_heads)), row(b_dw[0]), row(g_cln[0]), row(b_cln[0]), w_out[0].astype(BF16),
            row(g_mlp[0]), w_up[0].astype(BF16), w_down[0].astype(BF16), row(g_ple[0]), w_ple_gate[0].astype(BF16),
            w_ple[0].astype(BF16), row(g_final))

    xp = x_prompt.reshape(bp * seq, d)
    qkv, z, ba, glu = _inproj(xp, row(g_mix[0]), wmain, wba, gp, n_heads, dn, cc, tm=tm_in)
    o_raw, ssm_p, qkv_p = _seq_prompt(qkv, ba, w_short[0], batch=bp, tt=tt)
    cpad = -(-(n_dw - 1) // SUBLANES) * SUBLANES
    y_prompt, conf_p = _out_call(
        functools.partial(_out_prompt_kernel, tm_out, seq // tm_out, ff_chunk), "out_prompt", xp,
        [o_raw, z, glu, p_prompt[0].reshape(bp * seq, -1)], (w_dw[0],) + tail, tm_out,
        jax.ShapeDtypeStruct((bp, n_dw - 1, cc), F32),
        pl.BlockSpec((1, n_dw - 1, cc), lambda i: (i // (seq // tm_out), 0, 0)),
        [pltpu.VMEM((cpad + tm_out + SUBLANES, cc), F32)])

    xs = jnp.swapaxes(x_sample, 0, 1).reshape(n_tok * bs, d)
    qkv, z, ba, glu = _inproj(xs, row(g_mix[0]), wmain, wba, gp, n_heads, dn, cc, tm=n_tok * bs)
    tmaj = lambda a: a.reshape(n_tok, bs, -1)
    o_raw, ssm_s, qkv_s = _seq_sample(jnp.swapaxes(state_qkv_conv[0], 0, 1), tmaj(qkv), tmaj(ba), state_ssm[0],
                                      w_short[0])
    ps = jnp.swapaxes(p_sample[0], 0, 1).reshape(n_tok * bs, -1)
    hc = jnp.swapaxes(state_conf_conv[0], 0, 1)
    ys, conf_s = _out_call(
        functools.partial(_out_sample_kernel, ff_chunk), "out_sample", xs,
        [o_raw, tmaj(z), tmaj(glu), ps], (w_dw[0], hc) + tail, n_tok * bs,
        jax.ShapeDtypeStruct(hc.shape, F32), pl.BlockSpec(hc.shape, lambda i: (0, 0, 0)), [])
    y_sample = jnp.swapaxes(ys.reshape(n_tok, bs, d), 0, 1)

    return (y_prompt.reshape(bp, seq, d), y_sample, ssm_p[None], qkv_p[None], conf_p[None], ssm_s[None],
            jnp.swapaxes(qkv_s, 0, 1)[None], jnp.swapaxes(conf_s, 0, 1)[None])
```

```python
import functools

import jax
import jax.numpy as jnp
from jax import lax
from jax.experimental import pallas as pl
from jax.experimental.pallas import tpu as pltpu

F32 = jnp.float32
BF16 = jnp.bfloat16
EPS = 1e-6
HEAD_DIM = 64
CHUNK = 64
GROUP_W = 256
HEADS_PER_GROUP = GROUP_W // HEAD_DIM
SUBLANES = 8
VMEM_LIMIT = 56 * 1024 * 1024


def _dot(a, b):
    return jnp.dot(a.astype(BF16), b.astype(BF16), preferred_element_type=F32)


def _dot_nt(a, b):
    return lax.dot_general(a.astype(BF16), b.astype(BF16), (((1,), (1,)), ((), ())), preferred_element_type=F32)


def _sigmoid(x):
    return 1.0 / (1.0 + jnp.exp(-x))


def _silu(x):
    return x * _sigmoid(x)


def _softplus(x):
    return jnp.maximum(x, 0.0) + jnp.log1p(jnp.exp(-jnp.abs(x)))


def _rms(x, gain):
    return x * lax.rsqrt(jnp.mean(x * x, axis=-1, keepdims=True) + EPS) * gain


def _split3(x):
    hi = x.astype(BF16)
    r = x - hi.astype(F32)
    mid = r.astype(BF16)
    lo = (r - mid.astype(F32)).astype(BF16)
    return hi, mid, lo


def _dot_exact(a01, x):
    hi, mid, lo = _split3(x)
    d = lambda p: jnp.dot(a01, p, preferred_element_type=F32)
    return d(hi) + (d(mid) + d(lo))


def _block_ones(n):
    shift = HEAD_DIM.bit_length() - 1
    r = lax.broadcasted_iota(jnp.int32, (n, n), 0) >> shift
    c = lax.broadcasted_iota(jnp.int32, (n, n), 1) >> shift
    return (r == c).astype(BF16)


def _group_sum(x, ones_bd):
    outs = [jnp.dot(x[:, GROUP_W * b:GROUP_W * (b + 1)].astype(BF16), ones_bd, preferred_element_type=F32)
            for b in range(x.shape[1] // GROUP_W)]
    return jnp.concatenate(outs, axis=1)


def _expand_heads(cols, off, n_heads):
    rows = cols.shape[0]
    lo_half = lax.broadcasted_iota(jnp.int32, (rows, 128), 1) < HEAD_DIM
    blocks = []
    for b in range(n_heads // 2):
        c0 = cols[:, off + 2 * b:off + 2 * b + 1]
        c1 = cols[:, off + 2 * b + 1:off + 2 * b + 2]
        blocks.append(jnp.where(lo_half, c0, c1))
    return jnp.concatenate(blocks, axis=1)


class _Masks:
    SHAPES = dict(strict=((CHUNK, GROUP_W), F32), causal=((CHUNK, GROUP_W), F32), eye_cat=((CHUNK, GROUP_W), F32),
                  bd=((GROUP_W, GROUP_W), BF16), bd_f32=((GROUP_W, GROUP_W), F32), tri=((CHUNK, CHUNK), BF16),
                  ones_cc=((CHUNK, CHUNK), BF16))

    @classmethod
    def scratch(cls):
        return [pltpu.VMEM(shape, dtype) for shape, dtype in cls.SHAPES.values()]

    def __init__(self, refs):
        self._refs = dict(zip(self.SHAPES, refs))

    def __getattr__(self, name):
        return self._refs[name][...]

    def fill(self, seqs):
        c = CHUNK
        r = self._refs
        ri = lax.broadcasted_iota(jnp.int32, (c, GROUP_W), 0)
        lj = lax.broadcasted_iota(jnp.int32, (c, GROUP_W), 1) & (HEAD_DIM - 1)
        same = (ri & (seqs - 1)) == (lj & (seqs - 1))
        r["strict"][...] = (same & (ri > lj)).astype(F32)
        r["causal"][...] = (same & (ri >= lj)).astype(F32)
        r["eye_cat"][...] = (ri == lj).astype(F32)
        r["bd"][...] = _block_ones(GROUP_W)
        r["bd_f32"][...] = _block_ones(GROUP_W).astype(F32)
        r3 = lax.broadcasted_iota(jnp.int32, (c, c), 0)
        c3 = lax.broadcasted_iota(jnp.int32, (c, c), 1)
        r["tri"][...] = (((r3 & (seqs - 1)) == (c3 & (seqs - 1))) & (r3 >= c3)).astype(BF16)
        r["ones_cc"][...] = jnp.ones((c, c), BF16)


def _block_diag(y, m):
    return jnp.concatenate([y.astype(BF16)] * HEADS_PER_GROUP, axis=0) * m.bd


def _interleave(gens, weights):
    live = dict(zip(gens, weights))
    while live:
        for g, w in tuple(live.items()):
            for _ in range(w):
                try:
                    next(g)
                except StopIteration:
                    live.pop(g, None)
                    break


def _zero_from(v, width):
    bits = lax.bitcast_convert_type(v[0:1], jnp.int32)
    half = jnp.full(bits.shape, 16, jnp.int32)
    z = lax.shift_right_logical(lax.shift_right_logical(bits, half), half).astype(F32)
    return jnp.concatenate([z] * (width // z.shape[1]), axis=1)


def _drain(gen):
    for _ in gen:
        pass


def _prep_stage(out, acc, ba, g_last_of, dn, n_heads, m):
    qkv = _silu(acc)
    yield
    q, k, v = qkv[:, :dn], qkv[:, dn:2 * dn], qkv[:, 2 * dn:]
    qs = q * (lax.rsqrt(_group_sum(q * q, m.bd) + EPS) * (HEAD_DIM ** -0.5))
    k = k * lax.rsqrt(_group_sum(k * k, m.bd) + EPS)
    yield
    gcs = _dot_exact(m.tri, ba)
    bexp = _expand_heads(ba, 0, n_heads)
    g_row = _expand_heads(gcs, n_heads, n_heads)
    g_last = g_last_of(g_row)
    yield
    egc = jnp.exp(g_row)
    kb = k * bexp
    vb = v * bexp
    kbd = kb * egc
    qd = qs * egc
    kd = k * jnp.exp(g_last - g_row)
    yield
    for grp in range(dn // GROUP_W):
        sl = slice(GROUP_W * grp, GROUP_W * (grp + 1))
        gr = g_row[:, sl]
        g_col = _dot_exact(m.ones_cc, gr * m.eye_cat)
        dec = jnp.exp(jnp.minimum(gr - g_col, 0.0)) * m.causal
        aq = _dot_nt(jnp.concatenate([kb[:, sl], qs[:, sl]], axis=0), _block_diag(k[:, sl], m))
        out.append(dict(a=aq[:CHUNK] * (dec * m.strict), qk=aq[CHUNK:] * dec,
                        rhs=jnp.concatenate([_block_diag(vb[:, sl], m), _block_diag(kbd[:, sl], m)], axis=1),
                        qd=qd[:, sl], kd=kd[:, sl], g_last=g_last[:, sl]))
        yield


def _solve_stage(probs, m, levels):
    pws = [-p["a"] for p in probs]
    ts = [m.eye_cat + pw for pw in pws]
    for _ in range(levels):
        pws = [_dot(pw, _block_diag(pw, m)) for pw in pws]
        ts = [t + _dot(t, _block_diag(pw, m)) for t, pw in zip(ts, pws)]
        yield
    for p, t in zip(probs, ts):
        uw = _dot(t, p["rhs"])
        p["u"], p["w"] = uw[:, :GROUP_W], uw[:, GROUP_W:]
    yield


def _inproj_kernel(n_heads, dn, cc, x_ref, gmix_ref, wmain_ref, wba_ref, gp_ref, qkv_ref, z_ref, ba_ref, glu_ref):
    hb = _rms(x_ref[...], gmix_ref[...]).astype(BF16)
    d = lambda w: jnp.dot(hb, w, preferred_element_type=F32)
    o_z = 3 * dn
    o_ga = o_z + dn
    o_gb = o_ga + cc
    qkv_ref[...] = d(wmain_ref[:, :o_z])
    z_ref[...] = _silu(d(wmain_ref[:, o_z:o_ga]))
    glu_ref[...] = d(wmain_ref[:, o_ga:o_gb]) * _sigmoid(d(wmain_ref[:, o_gb:o_gb + cc]))
    ba = d(wba_ref[...])
    col = lax.broadcasted_iota(jnp.int32, ba.shape, 1)
    g = -jnp.exp(gp_ref[0:1, :]) * _softplus(ba + gp_ref[1:2, :])
    ba_ref[...] = jnp.where(col < n_heads, _sigmoid(ba), g)


def _inproj(x, gmix, wmain, wba, gp, n_heads, dn, cc, tm):
    t, d = x.shape
    const = lambda shape: pl.BlockSpec(shape, lambda i: (0, 0), pipeline_mode=pl.Buffered(1))
    row = lambda w: pl.BlockSpec((tm, w), lambda i: (i, 0))
    return pl.pallas_call(
        functools.partial(_inproj_kernel, n_heads, dn, cc),
        grid=(t // tm,),
        in_specs=[row(d), const(gmix.shape), const(wmain.shape), const(wba.shape), const(gp.shape)],
        out_specs=[row(3 * dn), row(dn), row(128), row(cc)],
        out_shape=[jax.ShapeDtypeStruct((t, 3 * dn), F32), jax.ShapeDtypeStruct((t, dn), F32),
                   jax.ShapeDtypeStruct((t, 128), F32), jax.ShapeDtypeStruct((t, cc), F32)],
        compiler_params=pltpu.CompilerParams(dimension_semantics=("arbitrary",), vmem_limit_bytes=VMEM_LIMIT),
        name="inproj",
    )(x, gmix, wmain, wba, gp)


def _seq_prompt_kernel(tt, dn, n_short, qkv_ref, ba_ref, wshort_ref, o_ref, sout_ref, qh_ref, xbuf, sbd, *mask_refs):
    step = pl.program_id(1)
    xpad = xbuf.shape[0] - tt
    n_heads = dn // HEAD_DIM
    m = _Masks(mask_refs)

    @pl.when((pl.program_id(0) == 0) & (step == 0))
    def _():
        m.fill(1)

    @pl.when(step == 0)
    def _():
        xbuf[0:xpad] = jnp.zeros((xpad, xbuf.shape[1]), F32)
        sbd[...] = jnp.zeros(sbd.shape, F32)

    xbuf[xpad:xpad + tt] = qkv_ref[...]
    levels = (CHUNK - 1).bit_length() - 1
    n_chunks = tt // CHUNK
    probs = [[] for _ in range(n_chunks)]

    def prep(c):
        lo = CHUNK * c
        window = xbuf[lo:lo + xpad + CHUNK]
        acc = window * wshort_ref[0:1]
        for j in range(1, n_short):
            acc = pltpu.roll(acc, 1, 0) + window * wshort_ref[j:j + 1]
        yield
        yield from _prep_stage(probs[c], acc[xpad:], ba_ref[lo:lo + CHUNK],
                               lambda g_row: g_row[CHUNK - 1:CHUNK], dn, n_heads, m)

    def recur(c):
        o_parts = []
        for grp, p in enumerate(probs[c]):
            s = sbd[grp]
            wq = _dot(jnp.concatenate([p["w"], p["qd"]], axis=0), s)
            v_new = p["u"] - wq[:CHUNK]
            yield
            o_parts.append(wq[CHUNK:] + _dot(p["qk"], _block_diag(v_new, m)))
            sbd[grp] = s * jnp.exp(p["g_last"]) + _dot(p["kd"].T, v_new) * m.bd_f32
            yield
        o_ref[CHUNK * c:CHUNK * (c + 1)] = jnp.concatenate(o_parts, axis=1)

    for c in range(n_chunks):
        _drain(prep(c))
    _drain(_solve_stage([p for pc in probs for p in pc], m, levels))
    for c in range(n_chunks):
        _drain(recur(c))

    xbuf[0:xpad] = xbuf[tt:tt + xpad]
    qh_ref[0] = xbuf[xpad - (n_short - 1):xpad]

    @pl.when(step == pl.num_programs(1) - 1)
    def _():
        for h in range(n_heads):
            grp, hl = divmod(h, HEADS_PER_GROUP)
            sout_ref[0, h] = sbd[grp, HEAD_DIM * hl:HEAD_DIM * (hl + 1), HEAD_DIM * hl:HEAD_DIM * (hl + 1)]


def _seq_prompt(qkv, ba, wshort, batch, tt):
    t, dn3 = qkv.shape
    dn = dn3 // 3
    nt = t // batch // tt
    n_heads = dn // HEAD_DIM
    n_short = wshort.shape[0]
    xpad = -(-(n_short - 1) // SUBLANES) * SUBLANES
    row = lambda w: pl.BlockSpec((tt, w), lambda b, i: (b * nt + i, 0))
    return pl.pallas_call(
        functools.partial(_seq_prompt_kernel, tt, dn, n_short),
        grid=(batch, nt),
        in_specs=[row(dn3), row(128), pl.BlockSpec(wshort.shape, lambda b, i: (0, 0))],
        out_specs=[row(dn),
                   pl.BlockSpec((1, n_heads, HEAD_DIM, HEAD_DIM), lambda b, i: (b, 0, 0, 0)),
                   pl.BlockSpec((1, n_short - 1, dn3), lambda b, i: (b, 0, 0))],
        out_shape=[jax.ShapeDtypeStruct((t, dn), F32),
                   jax.ShapeDtypeStruct((batch, n_heads, HEAD_DIM, HEAD_DIM), F32),
                   jax.ShapeDtypeStruct((batch, n_short - 1, dn3), F32)],
        scratch_shapes=[pltpu.VMEM((xpad + tt, dn3), F32), pltpu.VMEM((dn // GROUP_W, GROUP_W, GROUP_W), F32)]
        + _Masks.scratch(),
        compiler_params=pltpu.CompilerParams(dimension_semantics=("arbitrary", "arbitrary"),
                                             vmem_limit_bytes=VMEM_LIMIT),
        name="seq_prompt",
    )(qkv, ba, wshort)


def _seq_sample_kernel(dn, hq_ref, xq_ref, ba_ref, s0_ref, wshort_ref, o_ref, sout_ref, nq_ref, *mask_refs):
    n_tok, seqs = xq_ref.shape[0], xq_ref.shape[1]
    n_heads = dn // HEAD_DIM
    n_short = wshort_ref.shape[0]
    m = _Masks(mask_refs)
    levels = max((n_tok - 1).bit_length() - 1, 0)

    @pl.when(pl.program_id(0) == 0)
    def _():
        m.fill(seqs)

    full_q = [hq_ref[j] for j in range(n_short - 1)] + [xq_ref[t] for t in range(n_tok)]
    for j in range(n_short - 1):
        nq_ref[j] = full_q[n_tok + j]
    rows = []
    for t in range(n_tok):
        acc = full_q[t] * wshort_ref[0:1]
        for j in range(1, n_short):
            acc = acc + full_q[t + j] * wshort_ref[j:j + 1]
        rows.append(acc)
    ba = jnp.concatenate([ba_ref[t] for t in range(n_tok)], axis=0)
    probs = []
    g_last_of = lambda g_row: jnp.concatenate([g_row[(n_tok - 1) * seqs:]] * n_tok, axis=0)
    _drain(_prep_stage(probs, jnp.concatenate(rows, axis=0), ba, g_last_of, dn, n_heads, m))
    _drain(_solve_stage(probs, m, levels))

    row_seq = lax.broadcasted_iota(jnp.int32, (2 * CHUNK, GROUP_W), 0) & (seqs - 1)
    col_seq = lax.broadcasted_iota(jnp.int32, (GROUP_W, CHUNK), 1) & (seqs - 1)
    o_parts = []
    for grp, p in enumerate(probs):
        lhs = jnp.concatenate([p["w"], p["qd"]], axis=0).astype(BF16)
        s_bd = []
        wq = jnp.zeros((2 * CHUNK, GROUP_W), F32)
        for s in range(seqs):
            s_rows = s0_ref[s, HEADS_PER_GROUP * grp:HEADS_PER_GROUP * (grp + 1)].reshape(GROUP_W, HEAD_DIM)
            s_full = jnp.concatenate([s_rows] * HEADS_PER_GROUP, axis=1) * m.bd_f32
            s_bd.append(s_full)
            wq = jnp.where(row_seq == s, _dot(lhs, s_full), wq)
        v_new = p["u"] - wq[:CHUNK]
        o_parts.append(wq[CHUNK:] + _dot(p["qk"], _block_diag(v_new, m)))
        kd_t = p["kd"].T.astype(BF16)
        for s in range(seqs):
            upd = _dot(jnp.where(col_seq == s, kd_t, jnp.zeros((), BF16)), v_new)
            s_new = s_bd[s] * jnp.exp(p["g_last"][s:s + 1]) + upd * m.bd_f32
            for hl in range(HEADS_PER_GROUP):
                sout_ref[s, HEADS_PER_GROUP * grp + hl] = s_new[HEAD_DIM * hl:HEAD_DIM * (hl + 1),
                                                                 HEAD_DIM * hl:HEAD_DIM * (hl + 1)]
    o = jnp.concatenate(o_parts, axis=1)
    for t in range(n_tok):
        o_ref[t] = o[t * seqs:(t + 1) * seqs]


def _seq_sample(hq, xq, ba, s0, wshort):
    n_tok, nseq, dn3 = xq.shape
    dn = dn3 // 3
    seqs = CHUNK // n_tok
    n_heads = dn // HEAD_DIM
    tm = lambda a: pl.BlockSpec((a.shape[0], seqs, a.shape[2]), lambda i: (0, i, 0))
    st = pl.BlockSpec((seqs, n_heads, HEAD_DIM, HEAD_DIM), lambda i: (i, 0, 0, 0))
    o_shape = jax.ShapeDtypeStruct((n_tok, nseq, dn), F32)
    return pl.pallas_call(
        functools.partial(_seq_sample_kernel, dn),
        grid=(nseq // seqs,),
        in_specs=[tm(hq), tm(xq), tm(ba), st, pl.BlockSpec(wshort.shape, lambda i: (0, 0))],
        out_specs=[tm(o_shape), st, tm(hq)],
        out_shape=[o_shape, jax.ShapeDtypeStruct(s0.shape, F32), jax.ShapeDtypeStruct(hq.shape, F32)],
        scratch_shapes=_Masks.scratch(),
        compiler_params=pltpu.CompilerParams(dimension_semantics=("arbitrary",), vmem_limit_bytes=VMEM_LIMIT),
        name="seq_sample",
    )(hq, xq, ba, s0, wshort)


def _o_norm(o_raw, z_silu, go_ref, ones_bd):
    ms = _group_sum(o_raw * o_raw, ones_bd) * (1.0 / HEAD_DIM)
    return o_raw * lax.rsqrt(ms + EPS) * go_ref[...] * z_silu


def _conv_norm(conv, bdw_ref, gcln_ref, bcln_ref, ones_bd):
    inv = 1.0 / HEAD_DIM
    cv = conv + bdw_ref[...]
    d = cv - _group_sum(cv, ones_bd) * inv
    return _silu(d * lax.rsqrt(_group_sum(d * d, ones_bd) * inv + EPS) * gcln_ref[...] + bcln_ref[...])


def _fold_rows(x, prev=None):
    acc = x[0:SUBLANES] if prev is None else x[0:SUBLANES] + prev
    for r in range(SUBLANES, x.shape[0], SUBLANES):
        acc = acc + x[r:r + SUBLANES]
    return acc


def _tail_stage(y_ref, x, ocv, p, wout_ref, gmlp_ref, wup_ref, wdown_ref, gple_ref, wgate_ref, wple_ref, gfin_ref,
                ff_chunk, anchor=lambda: 0.0):
    def dot_cols(a, w_ref, r0=0, r1=None, c0=0, c1=None):
        a = a.astype(BF16)
        r1 = w_ref.shape[0] if r1 is None else r1
        c1 = w_ref.shape[1] if c1 is None else c1
        cols = []
        for c in range(c0, c1, GROUP_W):
            cols.append(jnp.dot(a, w_ref[r0:r1, c:c + GROUP_W], preferred_element_type=F32))
            yield
        return jnp.concatenate(cols, axis=1)

    x = x + (yield from dot_cols(ocv, wout_ref))
    hb = _rms(x, gmlp_ref[...] + anchor()).astype(BF16)
    mlp = None
    for f in range(wup_ref.shape[1] // ff_chunk):
        up = jnp.maximum((yield from dot_cols(hb, wup_ref, c0=f * ff_chunk, c1=(f + 1) * ff_chunk)), anchor())
        down = yield from dot_cols(up * up, wdown_ref, r0=f * ff_chunk, r1=(f + 1) * ff_chunk)
        mlp = down if mlp is None else mlp + down
    x = x + mlp
    gate = _sigmoid((yield from dot_cols(_rms(x, gple_ref[...] + anchor()), wgate_ref)))
    x = x + (yield from dot_cols(p, wple_ref)) * gate
    y_ref[...] = _rms(x, gfin_ref[...] + anchor())


def _mix_stage(progress, ocv_buf, nc_ref, o_ref, z_ref, glu_ref, wdw_ref, norm_refs, ones_ref, cbuf, tm, block):
    n_dw = wdw_ref.shape[0]
    cpad = cbuf.shape[0] - tm - SUBLANES
    first = cpad - (n_dw - 1)
    go_ref, bdw_ref, gcln_ref, bcln_ref = norm_refs
    cbuf[cpad:cpad + tm] = glu_ref[...]
    o = _o_norm(o_ref[...], z_ref[...], go_ref, ones_ref[...])
    progress.append(_fold_rows(o))
    yield
    blocks = []
    for r0 in range(0, tm, block):
        conv = None
        for s in range(SUBLANES):
            part = None
            for a in range((first + n_dw - 1) // SUBLANES + 1):
                j = SUBLANES * a + s - first
                if 0 <= j < n_dw:
                    lo = SUBLANES * a + r0
                    term = cbuf[lo:lo + block + SUBLANES] * wdw_ref[j:j + 1]
                    part = term if part is None else part + term
            if part is not None:
                conv = part[s:s + block] if conv is None else conv + part[s:s + block]
            yield
        blocks.append(conv)
        progress.append(_fold_rows(conv, progress[-1]))
    cbuf[0:cpad] = cbuf[tm:tm + cpad]
    nc_ref[0] = cbuf[cpad - (n_dw - 1):cpad]
    cv = _conv_norm(jnp.concatenate(blocks, axis=0), bdw_ref, gcln_ref, bcln_ref, ones_ref[...])
    progress.append(_fold_rows(cv, progress[-1]))
    ocv_buf[...] = jnp.concatenate([o, cv], axis=1).astype(BF16)


def _out_prompt_kernel(tm, tiles_per_seq, ff_chunk, x_ref, p_ref, o_ref, z_ref, glu_ref, wdw_ref, go_ref, bdw_ref,
                       gcln_ref, bcln_ref, *rest):
    *tail_refs, y_ref, nc_ref, cbuf, ocv_buf, ones_ref = rest
    step = pl.program_id(0)
    cpad = cbuf.shape[0] - tm - SUBLANES

    @pl.when(step == 0)
    def _():
        ones_ref[...] = _block_ones(GROUP_W)
        ocv_buf[...] = jnp.zeros(ocv_buf.shape, ocv_buf.dtype)
        cbuf[cpad + tm:] = jnp.zeros((SUBLANES, cbuf.shape[1]), F32)

    @pl.when(step % tiles_per_seq == 0)
    def _():
        cbuf[0:cpad] = jnp.zeros((cpad, cbuf.shape[1]), F32)

    progress = []
    anchor = lambda: _zero_from(progress[-1], x_ref.shape[1]) if progress else 0.0
    _interleave([_tail_stage(y_ref, x_ref[...], ocv_buf[...], p_ref[...], *tail_refs, ff_chunk, anchor),
                 _mix_stage(progress, ocv_buf, nc_ref, o_ref, z_ref, glu_ref, wdw_ref,
                            (go_ref, bdw_ref, gcln_ref, bcln_ref), ones_ref, cbuf, tm, CHUNK)], [2, 4])


def _out_sample_kernel(ff_chunk, x_ref, p_ref, o_ref, z_ref, glu_ref, wdw_ref, go_ref, bdw_ref, gcln_ref, bcln_ref,
                       hc_ref, *rest):
    *tail_refs, y_ref, nc_ref = rest
    n_tok = glu_ref.shape[0]
    n_dw = wdw_ref.shape[0]
    full_c = [hc_ref[j] for j in range(n_dw - 1)] + [glu_ref[t] for t in range(n_tok)]
    for j in range(n_dw - 1):
        nc_ref[j] = full_c[n_tok + j]
    rows = []
    for t in range(n_tok):
        acc = full_c[t] * wdw_ref[0:1]
        for j in range(1, n_dw):
            acc = acc + full_c[t + j] * wdw_ref[j:j + 1]
        rows.append(acc)
    cat = lambda ref: jnp.concatenate([ref[t] for t in range(n_tok)], axis=0)
    ones_bd = _block_ones(GROUP_W)
    ocv = jnp.concatenate([_o_norm(cat(o_ref), cat(z_ref), go_ref, ones_bd),
                           _conv_norm(jnp.concatenate(rows, axis=0), bdw_ref, gcln_ref, bcln_ref, ones_bd)],
                          axis=1).astype(BF16)
    _drain(_tail_stage(y_ref, x_ref[...], ocv, p_ref[...], *tail_refs, ff_chunk))


def _out_call(kernel_fn, name, x, behind, ahead, consts, tm, lag, extra_out_shape, extra_out_spec, scratch):
    t, d = x.shape
    nt = t // tm
    late = lambda i: jnp.maximum(i - lag, 0)
    early = lambda i: jnp.minimum(i, nt - 1)
    whole = lambda a, **kw: pl.BlockSpec(a.shape, lambda i: (0,) * a.ndim, **kw)
    tiled = lambda a, tile_of: (pl.BlockSpec((tm, a.shape[1]), lambda i: (tile_of(i), 0)) if a.ndim == 2
                                else whole(a))
    return pl.pallas_call(
        kernel_fn,
        grid=(nt + lag,),
        in_specs=[tiled(a, late) for a in (x, *behind)] + [tiled(a, early) for a in ahead]
        + [whole(a, pipeline_mode=pl.Buffered(1)) for a in consts],
        out_specs=[tiled(x, late), extra_out_spec],
        out_shape=[jax.ShapeDtypeStruct((t, d), F32), extra_out_shape],
        scratch_shapes=scratch,
        compiler_params=pltpu.CompilerParams(dimension_semantics=("arbitrary",), vmem_limit_bytes=VMEM_LIMIT),
        name=name,
    )(x, *behind, *ahead, *consts)


def kernel(x_prompt, x_sample, state_ssm, state_qkv_conv, state_conf_conv, p_prompt, p_sample, g_mix, w_in, w_short,
           a_log, dt_bias, g_o, w_dw, b_dw, g_cln, b_cln, w_out, g_mlp, w_up, w_down, w_ple, g_ple, w_ple_gate,
           g_final):
    depth = w_in.shape[0]
    assert depth == 1, "single-layer trunk"
    bp, seq, d = x_prompt.shape
    bs, n_tok, _ = x_sample.shape
    n_heads = a_log.shape[1]
    dn = n_heads * HEAD_DIM
    n_dw, cc = w_dw.shape[1], w_dw.shape[2]
    o_ba = 4 * dn
    o_glu = o_ba + 2 * n_heads
    assert dn % GROUP_W == 0 and cc % GROUP_W == 0 and 2 * n_heads <= 128 and CHUNK % n_tok == 0
    tm_in, tt, tm_out, ff_chunk = 512, 256, 512, 1024
    assert seq % tm_out == 0 and seq % tt == 0 and tt % CHUNK == 0 and (bp * seq) % tm_in == 0

    wi = w_in[0]
    wmain = jnp.concatenate([wi[:, :o_ba], wi[:, o_glu:]], axis=1).astype(BF16)
    wba = jnp.pad(wi[:, o_ba:o_glu], ((0, 0), (0, 128 - 2 * n_heads))).astype(BF16)
    gp = jnp.zeros((2, 128), F32).at[0, n_heads:2 * n_heads].set(a_log[0]).at[1, n_heads:2 * n_heads].set(dt_bias[0])
    row = lambda a: a.reshape(1, -1)
    norms = (row(jnp.tile(g_o[0], n_heads)), row(b_dw[0]), row(g_cln[0]), row(b_cln[0]))
    tail = (w_out[0].astype(BF16), row(g_mlp[0]), w_up[0].astype(BF16), w_down[0].astype(BF16), row(g_ple[0]),
            w_ple_gate[0].astype(BF16), w_ple[0].astype(BF16), row(g_final))

    xp = x_prompt.reshape(bp * seq, d)
    qkv, z, ba, glu = _inproj(xp, row(g_mix[0]), wmain, wba, gp, n_heads, dn, cc, tm=tm_in)
    o_raw, ssm_p, qkv_p = _seq_prompt(qkv, ba, w_short[0], batch=bp, tt=tt)
    cpad = -(-(n_dw - 1) // SUBLANES) * SUBLANES
    nt_out, tiles_per_seq = bp * seq // tm_out, seq // tm_out
    y_prompt, conf_p = _out_call(
        functools.partial(_out_prompt_kernel, tm_out, tiles_per_seq, ff_chunk), "out_prompt", xp,
        [p_prompt[0].reshape(bp * seq, -1)], [o_raw, z, glu], (w_dw[0],) + norms + tail, tm_out, 1,
        jax.ShapeDtypeStruct((bp, n_dw - 1, cc), F32),
        pl.BlockSpec((1, n_dw - 1, cc), lambda i: (jnp.minimum(i, nt_out - 1) // tiles_per_seq, 0, 0)),
        [pltpu.VMEM((cpad + tm_out + SUBLANES, cc), F32), pltpu.VMEM((tm_out, dn + cc), BF16),
         pltpu.VMEM((GROUP_W, GROUP_W), BF16)])

    xs = jnp.swapaxes(x_sample, 0, 1).reshape(n_tok * bs, d)
    qkv, z, ba, glu = _inproj(xs, row(g_mix[0]), wmain, wba, gp, n_heads, dn, cc, tm=n_tok * bs)
    tmaj = lambda a: a.reshape(n_tok, bs, -1)
    o_raw, ssm_s, qkv_s = _seq_sample(jnp.swapaxes(state_qkv_conv[0], 0, 1), tmaj(qkv), tmaj(ba), state_ssm[0],
                                      w_short[0])
    ps = jnp.swapaxes(p_sample[0], 0, 1).reshape(n_tok * bs, -1)
    hc = jnp.swapaxes(state_conf_conv[0], 0, 1)
    ys, conf_s = _out_call(
        functools.partial(_out_sample_kernel, ff_chunk), "out_sample", xs,
        [ps], [o_raw, tmaj(z), tmaj(glu)], (w_dw[0],) + norms + (hc,) + tail, n_tok * bs, 0,
        jax.ShapeDtypeStruct(hc.shape, F32), pl.BlockSpec(hc.shape, lambda i: (0, 0, 0)), [])
    y_sample = jnp.swapaxes(ys.reshape(n_tok, bs, d), 0, 1)

    return (y_prompt.reshape(bp, seq, d), y_sample, ssm_p[None], qkv_p[None], conf_p[None], ssm_s[None],
            jnp.swapaxes(qkv_s, 0, 1)[None], jnp.swapaxes(conf_s, 0, 1)[None])
```

```python
import functools

import jax
import jax.numpy as jnp
from jax import lax
from jax.experimental import pallas as pl
from jax.experimental.pallas import tpu as pltpu

F32 = jnp.float32
BF16 = jnp.bfloat16
EPS = 1e-6
HEAD_DIM = 64
CHUNK = 64
GROUP_W = 256
HEADS_PER_GROUP = GROUP_W // HEAD_DIM
SUBLANES = 8
VMEM_LIMIT = 56 * 1024 * 1024


def _dot(a, b):
    return jnp.dot(a.astype(BF16), b.astype(BF16), preferred_element_type=F32)


def _dot_nt(a, b):
    return lax.dot_general(a.astype(BF16), b.astype(BF16), (((1,), (1,)), ((), ())), preferred_element_type=F32)


def _sigmoid(x):
    return 1.0 / (1.0 + jnp.exp(-x))


def _silu(x):
    return x * _sigmoid(x)


def _softplus(x):
    return jnp.maximum(x, 0.0) + jnp.log1p(jnp.exp(-jnp.abs(x)))


def _rms(x, gain):
    return x * lax.rsqrt(jnp.mean(x * x, axis=-1, keepdims=True) + EPS) * gain


def _split3(x):
    hi = x.astype(BF16)
    r = x - hi.astype(F32)
    mid = r.astype(BF16)
    lo = (r - mid.astype(F32)).astype(BF16)
    return hi, mid, lo


def _dot_exact(a01, x):
    hi, mid, lo = _split3(x)
    d = lambda p: jnp.dot(a01, p, preferred_element_type=F32)
    return d(hi) + (d(mid) + d(lo))


def _block_ones(n):
    shift = HEAD_DIM.bit_length() - 1
    r = lax.broadcasted_iota(jnp.int32, (n, n), 0) >> shift
    c = lax.broadcasted_iota(jnp.int32, (n, n), 1) >> shift
    return (r == c).astype(BF16)


def _group_sum(x, ones_bd):
    outs = [jnp.dot(x[:, GROUP_W * b:GROUP_W * (b + 1)].astype(BF16), ones_bd, preferred_element_type=F32)
            for b in range(x.shape[1] // GROUP_W)]
    return jnp.concatenate(outs, axis=1)


def _expand_heads(cols, off, n_heads):
    rows = cols.shape[0]
    lo_half = lax.broadcasted_iota(jnp.int32, (rows, 128), 1) < HEAD_DIM
    blocks = []
    for b in range(n_heads // 2):
        c0 = cols[:, off + 2 * b:off + 2 * b + 1]
        c1 = cols[:, off + 2 * b + 1:off + 2 * b + 2]
        blocks.append(jnp.where(lo_half, c0, c1))
    return jnp.concatenate(blocks, axis=1)


class _Masks:
    SHAPES = dict(strict=((CHUNK, GROUP_W), F32), causal=((CHUNK, GROUP_W), F32), eye_cat=((CHUNK, GROUP_W), F32),
                  bd=((GROUP_W, GROUP_W), BF16), bd_f32=((GROUP_W, GROUP_W), F32), tri=((CHUNK, CHUNK), BF16),
                  ones_cc=((CHUNK, CHUNK), BF16))

    @classmethod
    def scratch(cls):
        return [pltpu.VMEM(shape, dtype) for shape, dtype in cls.SHAPES.values()]

    def __init__(self, refs, pool_ref=None):
        self._refs = dict(zip(self.SHAPES, refs))
        self.pool = _BlockDiagPool(pool_ref)

    def __getattr__(self, name):
        return self._refs[name][...]

    def fill(self, seqs):
        c = CHUNK
        r = self._refs
        ri = lax.broadcasted_iota(jnp.int32, (c, GROUP_W), 0)
        lj = lax.broadcasted_iota(jnp.int32, (c, GROUP_W), 1) & (HEAD_DIM - 1)
        same = (ri & (seqs - 1)) == (lj & (seqs - 1))
        r["strict"][...] = (same & (ri > lj)).astype(F32)
        r["causal"][...] = (same & (ri >= lj)).astype(F32)
        r["eye_cat"][...] = (ri == lj).astype(F32)
        r["bd"][...] = _block_ones(GROUP_W)
        r["bd_f32"][...] = _block_ones(GROUP_W).astype(F32)
        r3 = lax.broadcasted_iota(jnp.int32, (c, c), 0)
        c3 = lax.broadcasted_iota(jnp.int32, (c, c), 1)
        r["tri"][...] = (((r3 & (seqs - 1)) == (c3 & (seqs - 1))) & (r3 >= c3)).astype(BF16)
        r["ones_cc"][...] = jnp.ones((c, c), BF16)


class _BlockDiagPool:
    def __init__(self, ref):
        self.ref = ref
        self.used = 0

    def zero(self):
        self.ref[...] = jnp.zeros(self.ref.shape, self.ref.dtype)

    def expand(self, y):
        n = self.used
        self.used += 1
        yb = y.astype(BF16)
        for h in range(HEADS_PER_GROUP):
            blk = slice(HEAD_DIM * h, HEAD_DIM * (h + 1))
            self.ref[n, blk, blk] = yb[:, blk]
        return self.ref[n]


def _block_diag(y, m):
    return m.pool.expand(y)


def _interleave(gens, weights):
    live = dict(zip(gens, weights))
    while live:
        for g, w in tuple(live.items()):
            for _ in range(w):
                try:
                    next(g)
                except StopIteration:
                    live.pop(g, None)
                    break


def _zero_from(v, width):
    bits = lax.bitcast_convert_type(v[0:1], jnp.int32)
    half = jnp.full(bits.shape, 16, jnp.int32)
    z = lax.shift_right_logical(lax.shift_right_logical(bits, half), half).astype(F32)
    return jnp.concatenate([z] * (width // z.shape[1]), axis=1)


def _drain(gen):
    for _ in gen:
        pass


def _prep_stage(out, acc, ba, g_last_of, dn, n_heads, m):
    qkv = _silu(acc)
    yield
    q, k, v = qkv[:, :dn], qkv[:, dn:2 * dn], qkv[:, 2 * dn:]
    qs = q * (lax.rsqrt(_group_sum(q * q, m.bd) + EPS) * (HEAD_DIM ** -0.5))
    k = k * lax.rsqrt(_group_sum(k * k, m.bd) + EPS)
    yield
    gcs = _dot_exact(m.tri, ba)
    bexp = _expand_heads(ba, 0, n_heads)
    g_row = _expand_heads(gcs, n_heads, n_heads)
    g_last = g_last_of(g_row)
    yield
    egc = jnp.exp(g_row)
    kb = k * bexp
    vb = v * bexp
    kbd = kb * egc
    qd = qs * egc
    kd = k * jnp.exp(g_last - g_row)
    yield
    for grp in range(dn // GROUP_W):
        sl = slice(GROUP_W * grp, GROUP_W * (grp + 1))
        gr = g_row[:, sl]
        g_col = _dot_exact(m.ones_cc, gr * m.eye_cat)
        dec = jnp.exp(jnp.minimum(gr - g_col, 0.0)) * m.causal
        aq = _dot_nt(jnp.concatenate([kb[:, sl], qs[:, sl]], axis=0), _block_diag(k[:, sl], m))
        out.append(dict(a=aq[:CHUNK] * (dec * m.strict), qk=aq[CHUNK:] * dec,
                        rhs=jnp.concatenate([_block_diag(vb[:, sl], m), _block_diag(kbd[:, sl], m)], axis=1),
                        qd=qd[:, sl], kd=kd[:, sl], g_last=g_last[:, sl]))
        yield


def _solve_stage(probs, m, levels):
    pws = [-p["a"] for p in probs]
    ts = [m.eye_cat + pw for pw in pws]
    for _ in range(levels):
        pws = [_dot(pw, _block_diag(pw, m)) for pw in pws]
        ts = [t + _dot(t, _block_diag(pw, m)) for t, pw in zip(ts, pws)]
        yield
    for p, t in zip(probs, ts):
        uw = _dot(t, p["rhs"])
        p["u"], p["w"] = uw[:, :GROUP_W], uw[:, GROUP_W:]
    yield


def _inproj_kernel(n_heads, dn, cc, x_ref, gmix_ref, wmain_ref, wba_ref, gp_ref, qkv_ref, z_ref, ba_ref, glu_ref):
    hb = _rms(x_ref[...], gmix_ref[...]).astype(BF16)
    d = lambda w: jnp.dot(hb, w, preferred_element_type=F32)
    o_z = 3 * dn
    o_ga = o_z + dn
    o_gb = o_ga + cc
    qkv_ref[...] = d(wmain_ref[:, :o_z])
    z_ref[...] = _silu(d(wmain_ref[:, o_z:o_ga]))
    glu_ref[...] = d(wmain_ref[:, o_ga:o_gb]) * _sigmoid(d(wmain_ref[:, o_gb:o_gb + cc]))
    ba = d(wba_ref[...])
    col = lax.broadcasted_iota(jnp.int32, ba.shape, 1)
    g = -jnp.exp(gp_ref[0:1, :]) * _softplus(ba + gp_ref[1:2, :])
    ba_ref[...] = jnp.where(col < n_heads, _sigmoid(ba), g)


def _inproj(x, gmix, wmain, wba, gp, n_heads, dn, cc, tm):
    t, d = x.shape
    const = lambda shape: pl.BlockSpec(shape, lambda i: (0, 0), pipeline_mode=pl.Buffered(1))
    row = lambda w: pl.BlockSpec((tm, w), lambda i: (i, 0))
    return pl.pallas_call(
        functools.partial(_inproj_kernel, n_heads, dn, cc),
        grid=(t // tm,),
        in_specs=[row(d), const(gmix.shape), const(wmain.shape), const(wba.shape), const(gp.shape)],
        out_specs=[row(3 * dn), row(dn), row(128), row(cc)],
        out_shape=[jax.ShapeDtypeStruct((t, 3 * dn), F32), jax.ShapeDtypeStruct((t, dn), F32),
                   jax.ShapeDtypeStruct((t, 128), F32), jax.ShapeDtypeStruct((t, cc), F32)],
        compiler_params=pltpu.CompilerParams(dimension_semantics=("arbitrary",), vmem_limit_bytes=VMEM_LIMIT),
        name="inproj",
    )(x, gmix, wmain, wba, gp)


def _seq_prompt_kernel(dn, n_short, qkv_ref, ba_ref, wshort_ref, o_ref, sout_ref, qh_ref, hist, sbd, pool_ref,
                       *mask_refs):
    step = pl.program_id(0)
    n_seq = qkv_ref.shape[0]
    xpad = hist.shape[1]
    n_heads = dn // HEAD_DIM
    n_groups = dn // GROUP_W
    m = _Masks(mask_refs, pool_ref)
    levels = (CHUNK - 1).bit_length() - 1

    @pl.when(step == 0)
    def _():
        m.fill(1)
        m.pool.zero()
        hist[...] = jnp.zeros(hist.shape, F32)
        sbd[...] = jnp.zeros(sbd.shape, F32)

    probs = [[] for _ in range(n_seq)]

    def prep(b):
        window = jnp.concatenate([hist[b], qkv_ref[b]], axis=0)
        acc = window * wshort_ref[0:1]
        for j in range(1, n_short):
            acc = pltpu.roll(acc, 1, 0) + window * wshort_ref[j:j + 1]
        yield
        yield from _prep_stage(probs[b], acc[xpad:], ba_ref[b], lambda g_row: g_row[CHUNK - 1:CHUNK], dn, n_heads, m)

    def recur(b):
        o_parts = []
        for grp, p in enumerate(probs[b]):
            s = sbd[n_groups * b + grp]
            wq = _dot(jnp.concatenate([p["w"], p["qd"]], axis=0), s)
            v_new = p["u"] - wq[:CHUNK]
            yield
            o_parts.append(wq[CHUNK:] + _dot(p["qk"], _block_diag(v_new, m)))
            sbd[n_groups * b + grp] = s * jnp.exp(p["g_last"]) + _dot(p["kd"].T, v_new) * m.bd_f32
            yield
        o_ref[b] = jnp.concatenate(o_parts, axis=1)

    for b in range(n_seq):
        _drain(prep(b))
    _drain(_solve_stage([p for pb in probs for p in pb], m, levels))
    _interleave([recur(b) for b in range(n_seq)], [1] * n_seq)

    for b in range(n_seq):
        tail = qkv_ref[b, CHUNK - xpad:CHUNK]
        hist[b] = tail
        qh_ref[b] = tail[xpad - (n_short - 1):]

    @pl.when(step == pl.num_programs(0) - 1)
    def _():
        for b in range(n_seq):
            for h in range(n_heads):
                grp, hl = divmod(h, HEADS_PER_GROUP)
                sout_ref[b, h] = sbd[n_groups * b + grp, HEAD_DIM * hl:HEAD_DIM * (hl + 1),
                                     HEAD_DIM * hl:HEAD_DIM * (hl + 1)]


def _bd_uses(n_chunks, n_groups, levels=(CHUNK - 1).bit_length() - 1):
    return n_chunks * n_groups * (1 + 2 + 2 * levels + 1)


def _seq_prompt(qkv, ba, wshort, batch):
    t, dn3 = qkv.shape
    dn = dn3 // 3
    seq = t // batch
    n_heads = dn // HEAD_DIM
    n_groups = dn // GROUP_W
    n_short = wshort.shape[0]
    xpad = -(-(n_short - 1) // SUBLANES) * SUBLANES
    chunk_of = lambda w: pl.BlockSpec((batch, CHUNK, w), lambda c: (0, c, 0))
    whole = lambda shape: pl.BlockSpec(shape, lambda c: (0,) * len(shape))
    o, sout, qh = pl.pallas_call(
        functools.partial(_seq_prompt_kernel, dn, n_short),
        grid=(seq // CHUNK,),
        in_specs=[chunk_of(dn3), chunk_of(128), whole(wshort.shape)],
        out_specs=[chunk_of(dn), whole((batch, n_heads, HEAD_DIM, HEAD_DIM)), whole((batch, n_short - 1, dn3))],
        out_shape=[jax.ShapeDtypeStruct((batch, seq, dn), F32),
                   jax.ShapeDtypeStruct((batch, n_heads, HEAD_DIM, HEAD_DIM), F32),
                   jax.ShapeDtypeStruct((batch, n_short - 1, dn3), F32)],
        scratch_shapes=[pltpu.VMEM((batch, xpad, dn3), F32), pltpu.VMEM((batch * n_groups, GROUP_W, GROUP_W), F32),
                        pltpu.VMEM((_bd_uses(batch, n_groups), GROUP_W, GROUP_W), BF16)] + _Masks.scratch(),
        compiler_params=pltpu.CompilerParams(dimension_semantics=("arbitrary",), vmem_limit_bytes=VMEM_LIMIT),
        name="seq_prompt",
    )(qkv.reshape(batch, seq, dn3), ba.reshape(batch, seq, 128), wshort)
    return o.reshape(t, dn), sout, qh


def _seq_sample_kernel(dn, hq_ref, xq_ref, ba_ref, s0_ref, wshort_ref, o_ref, sout_ref, nq_ref, pool_ref, *mask_refs):
    n_tok, seqs = xq_ref.shape[0], xq_ref.shape[1]
    n_heads = dn // HEAD_DIM
    n_short = wshort_ref.shape[0]
    m = _Masks(mask_refs, pool_ref)
    levels = max((n_tok - 1).bit_length() - 1, 0)

    @pl.when(pl.program_id(0) == 0)
    def _():
        m.fill(seqs)
        m.pool.zero()

    full_q = [hq_ref[j] for j in range(n_short - 1)] + [xq_ref[t] for t in range(n_tok)]
    for j in range(n_short - 1):
        nq_ref[j] = full_q[n_tok + j]
    rows = []
    for t in range(n_tok):
        acc = full_q[t] * wshort_ref[0:1]
        for j in range(1, n_short):
            acc = acc + full_q[t + j] * wshort_ref[j:j + 1]
        rows.append(acc)
    ba = jnp.concatenate([ba_ref[t] for t in range(n_tok)], axis=0)
    probs = []
    g_last_of = lambda g_row: jnp.concatenate([g_row[(n_tok - 1) * seqs:]] * n_tok, axis=0)
    _drain(_prep_stage(probs, jnp.concatenate(rows, axis=0), ba, g_last_of, dn, n_heads, m))
    _drain(_solve_stage(probs, m, levels))

    row_seq = lax.broadcasted_iota(jnp.int32, (2 * CHUNK, GROUP_W), 0) & (seqs - 1)
    col_seq = lax.broadcasted_iota(jnp.int32, (GROUP_W, CHUNK), 1) & (seqs - 1)
    o_parts = []
    for grp, p in enumerate(probs):
        lhs = jnp.concatenate([p["w"], p["qd"]], axis=0).astype(BF16)
        s_bd = []
        wq = jnp.zeros((2 * CHUNK, GROUP_W), F32)
        for s in range(seqs):
            s_rows = s0_ref[s, HEADS_PER_GROUP * grp:HEADS_PER_GROUP * (grp + 1)].reshape(GROUP_W, HEAD_DIM)
            s_full = jnp.concatenate([s_rows] * HEADS_PER_GROUP, axis=1) * m.bd_f32
            s_bd.append(s_full)
            wq = jnp.where(row_seq == s, _dot(lhs, s_full), wq)
        v_new = p["u"] - wq[:CHUNK]
        o_parts.append(wq[CHUNK:] + _dot(p["qk"], _block_diag(v_new, m)))
        kd_t = p["kd"].T.astype(BF16)
        for s in range(seqs):
            upd = _dot(jnp.where(col_seq == s, kd_t, jnp.zeros((), BF16)), v_new)
            s_new = s_bd[s] * jnp.exp(p["g_last"][s:s + 1]) + upd * m.bd_f32
            for hl in range(HEADS_PER_GROUP):
                sout_ref[s, HEADS_PER_GROUP * grp + hl] = s_new[HEAD_DIM * hl:HEAD_DIM * (hl + 1),
                                                                 HEAD_DIM * hl:HEAD_DIM * (hl + 1)]
    o = jnp.concatenate(o_parts, axis=1)
    for t in range(n_tok):
        o_ref[t] = o[t * seqs:(t + 1) * seqs]


def _seq_sample(hq, xq, ba, s0, wshort):
    n_tok, nseq, dn3 = xq.shape
    dn = dn3 // 3
    seqs = CHUNK // n_tok
    n_heads = dn // HEAD_DIM
    tm = lambda a: pl.BlockSpec((a.shape[0], seqs, a.shape[2]), lambda i: (0, i, 0))
    st = pl.BlockSpec((seqs, n_heads, HEAD_DIM, HEAD_DIM), lambda i: (i, 0, 0, 0))
    o_shape = jax.ShapeDtypeStruct((n_tok, nseq, dn), F32)
    return pl.pallas_call(
        functools.partial(_seq_sample_kernel, dn),
        grid=(nseq // seqs,),
        in_specs=[tm(hq), tm(xq), tm(ba), st, pl.BlockSpec(wshort.shape, lambda i: (0, 0))],
        out_specs=[tm(o_shape), st, tm(hq)],
        out_shape=[o_shape, jax.ShapeDtypeStruct(s0.shape, F32), jax.ShapeDtypeStruct(hq.shape, F32)],
        scratch_shapes=[pltpu.VMEM((_bd_uses(1, dn // GROUP_W, max((n_tok - 1).bit_length() - 1, 0)), GROUP_W, GROUP_W), BF16)]
        + _Masks.scratch(),
        compiler_params=pltpu.CompilerParams(dimension_semantics=("arbitrary",), vmem_limit_bytes=VMEM_LIMIT),
        name="seq_sample",
    )(hq, xq, ba, s0, wshort)


def _o_norm(o_raw, z_silu, go_ref, ones_bd):
    ms = _group_sum(o_raw * o_raw, ones_bd) * (1.0 / HEAD_DIM)
    return o_raw * lax.rsqrt(ms + EPS) * go_ref[...] * z_silu


def _conv_norm(conv, bdw_ref, gcln_ref, bcln_ref, ones_bd):
    inv = 1.0 / HEAD_DIM
    cv = conv + bdw_ref[...]
    d = cv - _group_sum(cv, ones_bd) * inv
    return _silu(d * lax.rsqrt(_group_sum(d * d, ones_bd) * inv + EPS) * gcln_ref[...] + bcln_ref[...])


def _fold_rows(x, prev=None):
    acc = x[0:SUBLANES] if prev is None else x[0:SUBLANES] + prev
    for r in range(SUBLANES, x.shape[0], SUBLANES):
        acc = acc + x[r:r + SUBLANES]
    return acc


def _tail_stage(y_ref, x, ocv, p, wout_ref, gmlp_ref, wup_ref, wdown_ref, gple_ref, wgate_ref, wple_ref, gfin_ref,
                ff_chunk, anchor=lambda: 0.0):
    def dot_cols(a, w_ref, r0=0, r1=None, c0=0, c1=None):
        a = a.astype(BF16)
        r1 = w_ref.shape[0] if r1 is None else r1
        c1 = w_ref.shape[1] if c1 is None else c1
        cols = []
        for c in range(c0, c1, GROUP_W):
            cols.append(jnp.dot(a, w_ref[r0:r1, c:c + GROUP_W], preferred_element_type=F32))
            yield
        return jnp.concatenate(cols, axis=1)

    x = x + (yield from dot_cols(ocv, wout_ref))
    hb = _rms(x, gmlp_ref[...] + anchor()).astype(BF16)
    mlp = None
    for f in range(wup_ref.shape[1] // ff_chunk):
        up = jnp.maximum((yield from dot_cols(hb, wup_ref, c0=f * ff_chunk, c1=(f + 1) * ff_chunk)), anchor())
        down = yield from dot_cols(up * up, wdown_ref, r0=f * ff_chunk, r1=(f + 1) * ff_chunk)
        mlp = down if mlp is None else mlp + down
    x = x + mlp
    gate = _sigmoid((yield from dot_cols(_rms(x, gple_ref[...] + anchor()), wgate_ref)))
    x = x + (yield from dot_cols(p, wple_ref)) * gate
    y_ref[...] = _rms(x, gfin_ref[...] + anchor())


def _mix_stage(progress, ocv_buf, nc_ref, o_ref, z_ref, glu_ref, wdw_ref, norm_refs, ones_ref, cbuf, tm, block):
    n_dw = wdw_ref.shape[0]
    cpad = cbuf.shape[0] - tm - SUBLANES
    first = cpad - (n_dw - 1)
    go_ref, bdw_ref, gcln_ref, bcln_ref = norm_refs
    cbuf[cpad:cpad + tm] = glu_ref[...]
    o = _o_norm(o_ref[...], z_ref[...], go_ref, ones_ref[...])
    progress.append(_fold_rows(o))
    yield
    blocks = []
    for r0 in range(0, tm, block):
        conv = None
        for s in range(SUBLANES):
            part = None
            for a in range((first + n_dw - 1) // SUBLANES + 1):
                j = SUBLANES * a + s - first
                if 0 <= j < n_dw:
                    lo = SUBLANES * a + r0
                    term = cbuf[lo:lo + block + SUBLANES] * wdw_ref[j:j + 1]
                    part = term if part is None else part + term
            if part is not None:
                conv = part[s:s + block] if conv is None else conv + part[s:s + block]
            yield
        blocks.append(conv)
        progress.append(_fold_rows(conv, progress[-1]))
    cbuf[0:cpad] = cbuf[tm:tm + cpad]
    nc_ref[0] = cbuf[cpad - (n_dw - 1):cpad]
    cv = _conv_norm(jnp.concatenate(blocks, axis=0), bdw_ref, gcln_ref, bcln_ref, ones_ref[...])
    progress.append(_fold_rows(cv, progress[-1]))
    ocv_buf[...] = jnp.concatenate([o, cv], axis=1).astype(BF16)


def _out_prompt_kernel(tm, tiles_per_seq, ff_chunk, x_ref, p_ref, o_ref, z_ref, glu_ref, wdw_ref, go_ref, bdw_ref,
                       gcln_ref, bcln_ref, *rest):
    *tail_refs, y_ref, nc_ref, cbuf, ocv_buf, ones_ref = rest
    step = pl.program_id(0)
    cpad = cbuf.shape[0] - tm - SUBLANES

    @pl.when(step == 0)
    def _():
        ones_ref[...] = _block_ones(GROUP_W)
        ocv_buf[...] = jnp.zeros(ocv_buf.shape, ocv_buf.dtype)
        cbuf[cpad + tm:] = jnp.zeros((SUBLANES, cbuf.shape[1]), F32)

    @pl.when(step % tiles_per_seq == 0)
    def _():
        cbuf[0:cpad] = jnp.zeros((cpad, cbuf.shape[1]), F32)

    progress = []
    anchor = lambda: _zero_from(progress[-1], x_ref.shape[1]) if progress else 0.0
    _interleave([_tail_stage(y_ref, x_ref[...], ocv_buf[...], p_ref[...], *tail_refs, ff_chunk, anchor),
                 _mix_stage(progress, ocv_buf, nc_ref, o_ref, z_ref, glu_ref, wdw_ref,
                            (go_ref, bdw_ref, gcln_ref, bcln_ref), ones_ref, cbuf, tm, CHUNK)], [2, 4])


def _out_sample_kernel(ff_chunk, x_ref, p_ref, o_ref, z_ref, glu_ref, wdw_ref, go_ref, bdw_ref, gcln_ref, bcln_ref,
                       hc_ref, *rest):
    *tail_refs, y_ref, nc_ref = rest
    n_tok = glu_ref.shape[0]
    n_dw = wdw_ref.shape[0]
    full_c = [hc_ref[j] for j in range(n_dw - 1)] + [glu_ref[t] for t in range(n_tok)]
    for j in range(n_dw - 1):
        nc_ref[j] = full_c[n_tok + j]
    rows = []
    for t in range(n_tok):
        acc = full_c[t] * wdw_ref[0:1]
        for j in range(1, n_dw):
            acc = acc + full_c[t + j] * wdw_ref[j:j + 1]
        rows.append(acc)
    cat = lambda ref: jnp.concatenate([ref[t] for t in range(n_tok)], axis=0)
    ones_bd = _block_ones(GROUP_W)
    ocv = jnp.concatenate([_o_norm(cat(o_ref), cat(z_ref), go_ref, ones_bd),
                           _conv_norm(jnp.concatenate(rows, axis=0), bdw_ref, gcln_ref, bcln_ref, ones_bd)],
                          axis=1).astype(BF16)
    _drain(_tail_stage(y_ref, x_ref[...], ocv, p_ref[...], *tail_refs, ff_chunk))


def _out_call(kernel_fn, name, x, behind, ahead, consts, tm, lag, extra_out_shape, extra_out_spec, scratch):
    t, d = x.shape
    nt = t // tm
    late = lambda i: jnp.maximum(i - lag, 0)
    early = lambda i: jnp.minimum(i, nt - 1)
    whole = lambda a, **kw: pl.BlockSpec(a.shape, lambda i: (0,) * a.ndim, **kw)
    tiled = lambda a, tile_of: (pl.BlockSpec((tm, a.shape[1]), lambda i: (tile_of(i), 0)) if a.ndim == 2
                                else whole(a))
    return pl.pallas_call(
        kernel_fn,
        grid=(nt + lag,),
        in_specs=[tiled(a, late) for a in (x, *behind)] + [tiled(a, early) for a in ahead]
        + [whole(a, pipeline_mode=pl.Buffered(1)) for a in consts],
        out_specs=[tiled(x, late), extra_out_spec],
        out_shape=[jax.ShapeDtypeStruct((t, d), F32), extra_out_shape],
        scratch_shapes=scratch,
        compiler_params=pltpu.CompilerParams(dimension_semantics=("arbitrary",), vmem_limit_bytes=VMEM_LIMIT),
        name=name,
    )(x, *behind, *ahead, *consts)


def kernel(x_prompt, x_sample, state_ssm, state_qkv_conv, state_conf_conv, p_prompt, p_sample, g_mix, w_in, w_short,
           a_log, dt_bias, g_o, w_dw, b_dw, g_cln, b_cln, w_out, g_mlp, w_up, w_down, w_ple, g_ple, w_ple_gate,
           g_final):
    depth = w_in.shape[0]
    assert depth == 1, "single-layer trunk"
    bp, seq, d = x_prompt.shape
    bs, n_tok, _ = x_sample.shape
    n_heads = a_log.shape[1]
    dn = n_heads * HEAD_DIM
    n_dw, cc = w_dw.shape[1], w_dw.shape[2]
    o_ba = 4 * dn
    o_glu = o_ba + 2 * n_heads
    assert dn % GROUP_W == 0 and cc % GROUP_W == 0 and 2 * n_heads <= 128 and CHUNK % n_tok == 0
    tm_in, tm_out, ff_chunk = 512, 512, 1024
    assert seq % tm_out == 0 and seq % CHUNK == 0 and (bp * seq) % tm_in == 0

    wi = w_in[0]
    wmain = jnp.concatenate([wi[:, :o_ba], wi[:, o_glu:]], axis=1).astype(BF16)
    wba = jnp.pad(wi[:, o_ba:o_glu], ((0, 0), (0, 128 - 2 * n_heads))).astype(BF16)
    gp = jnp.zeros((2, 128), F32).at[0, n_heads:2 * n_heads].set(a_log[0]).at[1, n_heads:2 * n_heads].set(dt_bias[0])
    row = lambda a: a.reshape(1, -1)
    norms = (row(jnp.tile(g_o[0], n_heads)), row(b_dw[0]), row(g_cln[0]), row(b_cln[0]))
    tail = (w_out[0].astype(BF16), row(g_mlp[0]), w_up[0].astype(BF16), w_down[0].astype(BF16), row(g_ple[0]),
            w_ple_gate[0].astype(BF16), w_ple[0].astype(BF16), row(g_final))

    xp = x_prompt.reshape(bp * seq, d)
    qkv, z, ba, glu = _inproj(xp, row(g_mix[0]), wmain, wba, gp, n_heads, dn, cc, tm=tm_in)
    o_raw, ssm_p, qkv_p = _seq_prompt(qkv, ba, w_short[0], batch=bp)
    cpad = -(-(n_dw - 1) // SUBLANES) * SUBLANES
    nt_out, tiles_per_seq = bp * seq // tm_out, seq // tm_out
    y_prompt, conf_p = _out_call(
        functools.partial(_out_prompt_kernel, tm_out, tiles_per_seq, ff_chunk), "out_prompt", xp,
        [p_prompt[0].reshape(bp * seq, -1)], [o_raw, z, glu], (w_dw[0],) + norms + tail, tm_out, 1,
        jax.ShapeDtypeStruct((bp, n_dw - 1, cc), F32),
        pl.BlockSpec((1, n_dw - 1, cc), lambda i: (jnp.minimum(i, nt_out - 1) // tiles_per_seq, 0, 0)),
        [pltpu.VMEM((cpad + tm_out + SUBLANES, cc), F32), pltpu.VMEM((tm_out, dn + cc), BF16),
         pltpu.VMEM((GROUP_W, GROUP_W), BF16)])

    xs = jnp.swapaxes(x_sample, 0, 1).reshape(n_tok * bs, d)
    qkv, z, ba, glu = _inproj(xs, row(g_mix[0]), wmain, wba, gp, n_heads, dn, cc, tm=n_tok * bs)
    tmaj = lambda a: a.reshape(n_tok, bs, -1)
    o_raw, ssm_s, qkv_s = _seq_sample(jnp.swapaxes(state_qkv_conv[0], 0, 1), tmaj(qkv), tmaj(ba), state_ssm[0],
                                      w_short[0])
    ps = jnp.swapaxes(p_sample[0], 0, 1).reshape(n_tok * bs, -1)
    hc = jnp.swapaxes(state_conf_conv[0], 0, 1)
    ys, conf_s = _out_call(
        functools.partial(_out_sample_kernel, ff_chunk), "out_sample", xs,
        [ps], [o_raw, tmaj(z), tmaj(glu)], (w_dw[0],) + norms + (hc,) + tail, n_tok * bs, 0,
        jax.ShapeDtypeStruct(hc.shape, F32), pl.BlockSpec(hc.shape, lambda i: (0, 0, 0)), [])
    y_sample = jnp.swapaxes(ys.reshape(n_tok, bs, d), 0, 1)

    return (y_prompt.reshape(bp, seq, d), y_sample, ssm_p[None], qkv_p[None], conf_p[None], ssm_s[None],
            jnp.swapaxes(qkv_s, 0, 1)[None], jnp.swapaxes(conf_s, 0, 1)[None])
```

```python
import functools

import jax
import jax.numpy as jnp
from jax import lax
from jax.experimental import pallas as pl
from jax.experimental.pallas import tpu as pltpu

F32 = jnp.float32
BF16 = jnp.bfloat16
EPS = 1e-6
HEAD_DIM = 64
CHUNK = 64
GROUP_W = 256
HEADS_PER_GROUP = GROUP_W // HEAD_DIM
SUBLANES = 8
VMEM_LIMIT = 56 * 1024 * 1024


def _dot(a, b):
    return jnp.dot(a.astype(BF16), b.astype(BF16), preferred_element_type=F32)


def _dot_nt(a, b):
    return lax.dot_general(a.astype(BF16), b.astype(BF16), (((1,), (1,)), ((), ())), preferred_element_type=F32)


def _sigmoid(x):
    return 1.0 / (1.0 + jnp.exp(-x))


def _silu(x):
    return x * _sigmoid(x)


def _softplus(x):
    return jnp.maximum(x, 0.0) + jnp.log1p(jnp.exp(-jnp.abs(x)))


def _rms(x, gain):
    return x * lax.rsqrt(jnp.mean(x * x, axis=-1, keepdims=True) + EPS) * gain


def _split3(x):
    hi = x.astype(BF16)
    r = x - hi.astype(F32)
    mid = r.astype(BF16)
    lo = (r - mid.astype(F32)).astype(BF16)
    return hi, mid, lo


def _dot_exact(a01, x):
    hi, mid, lo = _split3(x)
    d = lambda p: jnp.dot(a01, p, preferred_element_type=F32)
    return d(hi) + (d(mid) + d(lo))


def _block_ones(n):
    shift = HEAD_DIM.bit_length() - 1
    r = lax.broadcasted_iota(jnp.int32, (n, n), 0) >> shift
    c = lax.broadcasted_iota(jnp.int32, (n, n), 1) >> shift
    return (r == c).astype(BF16)


def _group_sum(x, ones_bd):
    outs = [jnp.dot(x[:, GROUP_W * b:GROUP_W * (b + 1)].astype(BF16), ones_bd, preferred_element_type=F32)
            for b in range(x.shape[1] // GROUP_W)]
    return jnp.concatenate(outs, axis=1)


def _expand_heads(cols, off, n_heads):
    rows = cols.shape[0]
    lo_half = lax.broadcasted_iota(jnp.int32, (rows, 128), 1) < HEAD_DIM
    blocks = []
    for b in range(n_heads // 2):
        c0 = cols[:, off + 2 * b:off + 2 * b + 1]
        c1 = cols[:, off + 2 * b + 1:off + 2 * b + 2]
        blocks.append(jnp.where(lo_half, c0, c1))
    return jnp.concatenate(blocks, axis=1)


class _Masks:
    SHAPES = dict(strict=((CHUNK, GROUP_W), F32), causal=((CHUNK, GROUP_W), F32), eye_cat=((CHUNK, GROUP_W), F32),
                  bd=((GROUP_W, GROUP_W), BF16), bd_f32=((GROUP_W, GROUP_W), F32), tri=((CHUNK, CHUNK), BF16),
                  ones_cc=((CHUNK, CHUNK), BF16))

    @classmethod
    def scratch(cls):
        return [pltpu.VMEM(shape, dtype) for shape, dtype in cls.SHAPES.values()]

    def __init__(self, refs, pool_ref=None):
        self._refs = dict(zip(self.SHAPES, refs))
        self.pool = _BlockDiagPool(pool_ref)

    def __getattr__(self, name):
        return self._refs[name][...]

    def fill(self, seqs):
        c = CHUNK
        r = self._refs
        ri = lax.broadcasted_iota(jnp.int32, (c, GROUP_W), 0)
        lj = lax.broadcasted_iota(jnp.int32, (c, GROUP_W), 1) & (HEAD_DIM - 1)
        same = (ri & (seqs - 1)) == (lj & (seqs - 1))
        r["strict"][...] = (same & (ri > lj)).astype(F32)
        r["causal"][...] = (same & (ri >= lj)).astype(F32)
        r["eye_cat"][...] = (ri == lj).astype(F32)
        r["bd"][...] = _block_ones(GROUP_W)
        r["bd_f32"][...] = _block_ones(GROUP_W).astype(F32)
        r3 = lax.broadcasted_iota(jnp.int32, (c, c), 0)
        c3 = lax.broadcasted_iota(jnp.int32, (c, c), 1)
        r["tri"][...] = (((r3 & (seqs - 1)) == (c3 & (seqs - 1))) & (r3 >= c3)).astype(BF16)
        r["ones_cc"][...] = jnp.ones((c, c), BF16)


class _BlockDiagPool:
    def __init__(self, ref):
        self.ref = ref
        self.used = 0

    def zero(self):
        self.ref[...] = jnp.zeros(self.ref.shape, self.ref.dtype)

    def expand(self, y):
        n = self.used
        self.used += 1
        yb = y.astype(BF16)
        for h in range(HEADS_PER_GROUP):
            blk = slice(HEAD_DIM * h, HEAD_DIM * (h + 1))
            self.ref[n, blk, blk] = yb[:, blk]
        return self.ref[n]


def _block_diag(y, m):
    return m.pool.expand(y)


def _interleave(gens, weights):
    live = dict(zip(gens, weights))
    while live:
        for g, w in tuple(live.items()):
            for _ in range(w):
                try:
                    next(g)
                except StopIteration:
                    live.pop(g, None)
                    break


def _zero_from(v, width):
    bits = lax.bitcast_convert_type(v[0:1], jnp.int32)
    half = jnp.full(bits.shape, 16, jnp.int32)
    z = lax.shift_right_logical(lax.shift_right_logical(bits, half), half).astype(F32)
    return jnp.concatenate([z] * (width // z.shape[1]), axis=1)


def _drain(gen):
    for _ in gen:
        pass


def _prep_stage(out, acc, ba, g_last_of, dn, n_heads, m):
    qkv = _silu(acc)
    yield
    q, k, v = qkv[:, :dn], qkv[:, dn:2 * dn], qkv[:, 2 * dn:]
    qs = q * (lax.rsqrt(_group_sum(q * q, m.bd) + EPS) * (HEAD_DIM ** -0.5))
    k = k * lax.rsqrt(_group_sum(k * k, m.bd) + EPS)
    yield
    gcs = _dot_exact(m.tri, ba)
    bexp = _expand_heads(ba, 0, n_heads)
    g_row = _expand_heads(gcs, n_heads, n_heads)
    g_last = g_last_of(g_row)
    yield
    egc = jnp.exp(g_row)
    kb = k * bexp
    vb = v * bexp
    kbd = kb * egc
    qd = qs * egc
    kd = k * jnp.exp(g_last - g_row)
    yield
    for grp in range(dn // GROUP_W):
        sl = slice(GROUP_W * grp, GROUP_W * (grp + 1))
        gr = g_row[:, sl]
        g_col = _dot_exact(m.ones_cc, gr * m.eye_cat)
        dec = jnp.exp(jnp.minimum(gr - g_col, 0.0)) * m.causal
        aq = _dot_nt(jnp.concatenate([kb[:, sl], qs[:, sl]], axis=0), _block_diag(k[:, sl], m))
        out.append(dict(a=aq[:CHUNK] * (dec * m.strict), qk=aq[CHUNK:] * dec,
                        rhs=jnp.concatenate([_block_diag(vb[:, sl], m), _block_diag(kbd[:, sl], m)], axis=1),
                        qd=qd[:, sl], kd=kd[:, sl], g_last=g_last[:, sl]))
        yield


def _solve_stage(probs, m, levels):
    pws = [-p["a"] for p in probs]
    ts = [m.eye_cat + pw for pw in pws]
    for _ in range(levels):
        pws = [_dot(pw, _block_diag(pw, m)) for pw in pws]
        ts = [t + _dot(t, _block_diag(pw, m)) for t, pw in zip(ts, pws)]
        yield
    for p, t in zip(probs, ts):
        uw = _dot(t, p["rhs"])
        p["u"], p["w"] = uw[:, :GROUP_W], uw[:, GROUP_W:]
    yield


def _inproj_kernel(n_heads, dn, cc, x_ref, gmix_ref, wt_ref, gp_ref, qkv_ref, z_ref, ba_ref, glu_ref):
    hb = _rms(x_ref[...], gmix_ref[...]).astype(BF16)
    d = lambda r0, r1: lax.dot_general(hb, wt_ref[r0:r1], (((1,), (1,)), ((), ())), preferred_element_type=F32)
    o_z = 3 * dn
    o_ba = o_z + dn
    o_ga = o_ba + 2 * n_heads
    o_gb = o_ga + cc
    qkv_ref[...] = d(0, o_z)
    z_ref[...] = _silu(d(o_z, o_ba))
    glu_ref[...] = d(o_ga, o_gb) * _sigmoid(d(o_gb, o_gb + cc))
    ba = d(o_ba, o_ga)
    col = lax.broadcasted_iota(jnp.int32, ba.shape, 1)
    g = -jnp.exp(gp_ref[0:1, :]) * _softplus(ba + gp_ref[1:2, :])
    ba_ref[...] = jnp.where(col < n_heads, _sigmoid(ba), g)


def _inproj(x, gmix, wt, gp, n_heads, dn, cc, tm):
    t, d = x.shape
    const = lambda shape: pl.BlockSpec(shape, lambda i: (0, 0), pipeline_mode=pl.Buffered(1))
    row = lambda w: pl.BlockSpec((tm, w), lambda i: (i, 0))
    return pl.pallas_call(
        functools.partial(_inproj_kernel, n_heads, dn, cc),
        grid=(t // tm,),
        in_specs=[row(d), const(gmix.shape), const(wt.shape), const(gp.shape)],
        out_specs=[row(3 * dn), row(dn), row(2 * n_heads), row(cc)],
        out_shape=[jax.ShapeDtypeStruct((t, 3 * dn), F32), jax.ShapeDtypeStruct((t, dn), F32),
                   jax.ShapeDtypeStruct((t, 2 * n_heads), F32), jax.ShapeDtypeStruct((t, cc), F32)],
        compiler_params=pltpu.CompilerParams(dimension_semantics=("arbitrary",), vmem_limit_bytes=VMEM_LIMIT),
        name="inproj",
    )(x, gmix, wt, gp)


def _seq_prompt_kernel(dn, n_short, qkv_ref, ba_ref, wshort_ref, o_ref, sout_ref, qh_ref, hist, sbd, pool_ref,
                       *mask_refs):
    step = pl.program_id(0)
    n_seq = qkv_ref.shape[0]
    xpad = hist.shape[1]
    n_heads = dn // HEAD_DIM
    n_groups = dn // GROUP_W
    m = _Masks(mask_refs, pool_ref)
    levels = (CHUNK - 1).bit_length() - 1

    @pl.when(step == 0)
    def _():
        m.fill(1)
        m.pool.zero()
        hist[...] = jnp.zeros(hist.shape, F32)
        sbd[...] = jnp.zeros(sbd.shape, F32)

    probs = [[] for _ in range(n_seq)]

    def prep(b):
        window = jnp.concatenate([hist[b], qkv_ref[b]], axis=0)
        acc = window * wshort_ref[0:1]
        for j in range(1, n_short):
            acc = pltpu.roll(acc, 1, 0) + window * wshort_ref[j:j + 1]
        yield
        yield from _prep_stage(probs[b], acc[xpad:], ba_ref[b], lambda g_row: g_row[CHUNK - 1:CHUNK], dn, n_heads, m)

    def recur(b):
        o_parts = []
        for grp, p in enumerate(probs[b]):
            s = sbd[n_groups * b + grp]
            wq = _dot(jnp.concatenate([p["w"], p["qd"]], axis=0), s)
            v_new = p["u"] - wq[:CHUNK]
            yield
            o_parts.append(wq[CHUNK:] + _dot(p["qk"], _block_diag(v_new, m)))
            sbd[n_groups * b + grp] = s * jnp.exp(p["g_last"]) + _dot(p["kd"].T, v_new) * m.bd_f32
            yield
        o_ref[b] = jnp.concatenate(o_parts, axis=1)

    for b in range(n_seq):
        _drain(prep(b))
    _drain(_solve_stage([p for pb in probs for p in pb], m, levels))
    _interleave([recur(b) for b in range(n_seq)], [1] * n_seq)

    for b in range(n_seq):
        tail = qkv_ref[b, CHUNK - xpad:CHUNK]
        hist[b] = tail
        qh_ref[b] = tail[xpad - (n_short - 1):]

    @pl.when(step == pl.num_programs(0) - 1)
    def _():
        for b in range(n_seq):
            for h in range(n_heads):
                grp, hl = divmod(h, HEADS_PER_GROUP)
                sout_ref[b, h] = sbd[n_groups * b + grp, HEAD_DIM * hl:HEAD_DIM * (hl + 1),
                                     HEAD_DIM * hl:HEAD_DIM * (hl + 1)]


def _bd_uses(n_chunks, n_groups, levels=(CHUNK - 1).bit_length() - 1):
    return n_chunks * n_groups * (1 + 2 + 2 * levels + 1)


def _seq_prompt(qkv, ba, wshort, batch):
    t, dn3 = qkv.shape
    dn = dn3 // 3
    seq = t // batch
    n_heads = dn // HEAD_DIM
    n_groups = dn // GROUP_W
    n_short = wshort.shape[0]
    xpad = -(-(n_short - 1) // SUBLANES) * SUBLANES
    chunk_of = lambda w: pl.BlockSpec((batch, CHUNK, w), lambda c: (0, c, 0))
    whole = lambda shape: pl.BlockSpec(shape, lambda c: (0,) * len(shape))
    o, sout, qh = pl.pallas_call(
        functools.partial(_seq_prompt_kernel, dn, n_short),
        grid=(seq // CHUNK,),
        in_specs=[chunk_of(dn3), chunk_of(ba.shape[1]), whole(wshort.shape)],
        out_specs=[chunk_of(dn), whole((batch, n_heads, HEAD_DIM, HEAD_DIM)), whole((batch, n_short - 1, dn3))],
        out_shape=[jax.ShapeDtypeStruct((batch, seq, dn), F32),
                   jax.ShapeDtypeStruct((batch, n_heads, HEAD_DIM, HEAD_DIM), F32),
                   jax.ShapeDtypeStruct((batch, n_short - 1, dn3), F32)],
        scratch_shapes=[pltpu.VMEM((batch, xpad, dn3), F32), pltpu.VMEM((batch * n_groups, GROUP_W, GROUP_W), F32),
                        pltpu.VMEM((_bd_uses(batch, n_groups), GROUP_W, GROUP_W), BF16)] + _Masks.scratch(),
        compiler_params=pltpu.CompilerParams(dimension_semantics=("arbitrary",), vmem_limit_bytes=VMEM_LIMIT),
        name="seq_prompt",
    )(qkv.reshape(batch, seq, dn3), ba.reshape(batch, seq, ba.shape[1]), wshort)
    return o.reshape(t, dn), sout, qh


def _seq_sample_kernel(dn, hq_ref, xq_ref, ba_ref, s0_ref, wshort_ref, o_ref, sout_ref, nq_ref, pool_ref, *mask_refs):
    n_tok, seqs = xq_ref.shape[0], xq_ref.shape[1]
    n_heads = dn // HEAD_DIM
    n_short = wshort_ref.shape[0]
    m = _Masks(mask_refs, pool_ref)
    levels = max((n_tok - 1).bit_length() - 1, 0)

    @pl.when(pl.program_id(0) == 0)
    def _():
        m.fill(seqs)
        m.pool.zero()

    full_q = [hq_ref[j] for j in range(n_short - 1)] + [xq_ref[t] for t in range(n_tok)]
    for j in range(n_short - 1):
        nq_ref[j] = full_q[n_tok + j]
    rows = []
    for t in range(n_tok):
        acc = full_q[t] * wshort_ref[0:1]
        for j in range(1, n_short):
            acc = acc + full_q[t + j] * wshort_ref[j:j + 1]
        rows.append(acc)
    ba = jnp.concatenate([ba_ref[t] for t in range(n_tok)], axis=0)
    probs = []
    g_last_of = lambda g_row: jnp.concatenate([g_row[(n_tok - 1) * seqs:]] * n_tok, axis=0)
    _drain(_prep_stage(probs, jnp.concatenate(rows, axis=0), ba, g_last_of, dn, n_heads, m))
    _drain(_solve_stage(probs, m, levels))

    row_seq = lax.broadcasted_iota(jnp.int32, (2 * CHUNK, GROUP_W), 0) & (seqs - 1)
    col_seq = lax.broadcasted_iota(jnp.int32, (GROUP_W, CHUNK), 1) & (seqs - 1)
    o_parts = []
    for grp, p in enumerate(probs):
        lhs = jnp.concatenate([p["w"], p["qd"]], axis=0).astype(BF16)
        s_bd = []
        wq = jnp.zeros((2 * CHUNK, GROUP_W), F32)
        for s in range(seqs):
            s_rows = s0_ref[s, HEADS_PER_GROUP * grp:HEADS_PER_GROUP * (grp + 1)].reshape(GROUP_W, HEAD_DIM)
            s_full = jnp.concatenate([s_rows] * HEADS_PER_GROUP, axis=1) * m.bd_f32
            s_bd.append(s_full)
            wq = jnp.where(row_seq == s, _dot(lhs, s_full), wq)
        v_new = p["u"] - wq[:CHUNK]
        o_parts.append(wq[CHUNK:] + _dot(p["qk"], _block_diag(v_new, m)))
        kd_t = p["kd"].T.astype(BF16)
        for s in range(seqs):
            upd = _dot(jnp.where(col_seq == s, kd_t, jnp.zeros((), BF16)), v_new)
            s_new = s_bd[s] * jnp.exp(p["g_last"][s:s + 1]) + upd * m.bd_f32
            for hl in range(HEADS_PER_GROUP):
                sout_ref[s, HEADS_PER_GROUP * grp + hl] = s_new[HEAD_DIM * hl:HEAD_DIM * (hl + 1),
                                                                 HEAD_DIM * hl:HEAD_DIM * (hl + 1)]
    o = jnp.concatenate(o_parts, axis=1)
    for t in range(n_tok):
        o_ref[t] = o[t * seqs:(t + 1) * seqs]


def _seq_sample(hq, xq, ba, s0, wshort):
    n_tok, nseq, dn3 = xq.shape
    dn = dn3 // 3
    seqs = CHUNK // n_tok
    n_heads = dn // HEAD_DIM
    tm = lambda a: pl.BlockSpec((a.shape[0], seqs, a.shape[2]), lambda i: (0, i, 0))
    st = pl.BlockSpec((seqs, n_heads, HEAD_DIM, HEAD_DIM), lambda i: (i, 0, 0, 0))
    o_shape = jax.ShapeDtypeStruct((n_tok, nseq, dn), F32)
    return pl.pallas_call(
        functools.partial(_seq_sample_kernel, dn),
        grid=(nseq // seqs,),
        in_specs=[tm(hq), tm(xq), tm(ba), st, pl.BlockSpec(wshort.shape, lambda i: (0, 0))],
        out_specs=[tm(o_shape), st, tm(hq)],
        out_shape=[o_shape, jax.ShapeDtypeStruct(s0.shape, F32), jax.ShapeDtypeStruct(hq.shape, F32)],
        scratch_shapes=[pltpu.VMEM((_bd_uses(1, dn // GROUP_W, max((n_tok - 1).bit_length() - 1, 0)), GROUP_W, GROUP_W), BF16)]
        + _Masks.scratch(),
        compiler_params=pltpu.CompilerParams(dimension_semantics=("arbitrary",), vmem_limit_bytes=VMEM_LIMIT),
        name="seq_sample",
    )(hq, xq, ba, s0, wshort)


def _o_norm(o_raw, z_silu, go_ref, ones_bd):
    ms = _group_sum(o_raw * o_raw, ones_bd) * (1.0 / HEAD_DIM)
    return o_raw * lax.rsqrt(ms + EPS) * go_ref[...] * z_silu


def _conv_norm(conv, bdw_ref, gcln_ref, bcln_ref, ones_bd):
    inv = 1.0 / HEAD_DIM
    cv = conv + bdw_ref[...]
    d = cv - _group_sum(cv, ones_bd) * inv
    return _silu(d * lax.rsqrt(_group_sum(d * d, ones_bd) * inv + EPS) * gcln_ref[...] + bcln_ref[...])


def _fold_rows(x, prev=None):
    acc = x[0:SUBLANES] if prev is None else x[0:SUBLANES] + prev
    for r in range(SUBLANES, x.shape[0], SUBLANES):
        acc = acc + x[r:r + SUBLANES]
    return acc


def _tail_stage(y_ref, x, ocv, p, wout_ref, gmlp_ref, wup_ref, wdown_ref, gple_ref, wgate_ref, wple_ref, gfin_ref,
                ff_chunk, anchor=lambda: 0.0):
    def dot_cols(a, w_ref, r0=0, r1=None, c0=0, c1=None):
        a = a.astype(BF16)
        r1 = w_ref.shape[0] if r1 is None else r1
        c1 = w_ref.shape[1] if c1 is None else c1
        cols = []
        for c in range(c0, c1, GROUP_W):
            cols.append(jnp.dot(a, w_ref[r0:r1, c:c + GROUP_W], preferred_element_type=F32))
            yield
        return jnp.concatenate(cols, axis=1)

    x = x + (yield from dot_cols(ocv, wout_ref))
    hb = _rms(x, gmlp_ref[...] + anchor()).astype(BF16)
    mlp = None
    for f in range(wup_ref.shape[1] // ff_chunk):
        up = jnp.maximum((yield from dot_cols(hb, wup_ref, c0=f * ff_chunk, c1=(f + 1) * ff_chunk)), anchor())
        down = yield from dot_cols(up * up, wdown_ref, r0=f * ff_chunk, r1=(f + 1) * ff_chunk)
        mlp = down if mlp is None else mlp + down
    x = x + mlp
    gate = _sigmoid((yield from dot_cols(_rms(x, gple_ref[...] + anchor()), wgate_ref)))
    x = x + (yield from dot_cols(p, wple_ref)) * gate
    y_ref[...] = _rms(x, gfin_ref[...] + anchor())


def _mix_stage(progress, ocv_buf, nc_ref, o_ref, z_ref, glu_ref, wdw_ref, norm_refs, ones_ref, cbuf, tm, block):
    n_dw = wdw_ref.shape[0]
    cpad = cbuf.shape[0] - tm - SUBLANES
    first = cpad - (n_dw - 1)
    go_ref, bdw_ref, gcln_ref, bcln_ref = norm_refs
    cbuf[cpad:cpad + tm] = glu_ref[...]
    o = _o_norm(o_ref[...], z_ref[...], go_ref, ones_ref[...])
    progress.append(_fold_rows(o))
    yield
    blocks = []
    for r0 in range(0, tm, block):
        conv = None
        for s in range(SUBLANES):
            part = None
            for a in range((first + n_dw - 1) // SUBLANES + 1):
                j = SUBLANES * a + s - first
                if 0 <= j < n_dw:
                    lo = SUBLANES * a + r0
                    term = cbuf[lo:lo + block + SUBLANES] * wdw_ref[j:j + 1]
                    part = term if part is None else part + term
            if part is not None:
                conv = part[s:s + block] if conv is None else conv + part[s:s + block]
            yield
        blocks.append(conv)
        progress.append(_fold_rows(conv, progress[-1]))
    cbuf[0:cpad] = cbuf[tm:tm + cpad]
    nc_ref[0] = cbuf[cpad - (n_dw - 1):cpad]
    cv = _conv_norm(jnp.concatenate(blocks, axis=0), bdw_ref, gcln_ref, bcln_ref, ones_ref[...])
    progress.append(_fold_rows(cv, progress[-1]))
    ocv_buf[...] = jnp.concatenate([o, cv], axis=1).astype(BF16)


def _out_prompt_kernel(tm, tiles_per_seq, ff_chunk, x_ref, p_ref, o_ref, z_ref, glu_ref, wdw_ref, go_ref, bdw_ref,
                       gcln_ref, bcln_ref, *rest):
    *tail_refs, y_ref, nc_ref, cbuf, ocv_buf, ones_ref = rest
    step = pl.program_id(0)
    cpad = cbuf.shape[0] - tm - SUBLANES

    @pl.when(step == 0)
    def _():
        ones_ref[...] = _block_ones(GROUP_W)
        cbuf[cpad + tm:] = jnp.zeros((SUBLANES, cbuf.shape[1]), F32)

    @pl.when(step % tiles_per_seq == 0)
    def _():
        cbuf[0:cpad] = jnp.zeros((cpad, cbuf.shape[1]), F32)

    mix = lambda progress: _mix_stage(progress, ocv_buf, nc_ref, o_ref, z_ref, glu_ref, wdw_ref,
                                      (go_ref, bdw_ref, gcln_ref, bcln_ref), ones_ref, cbuf, tm, CHUNK)

    @pl.when(step == 0)
    def _():
        _drain(mix([]))

    @pl.when(step > 0)
    def _():
        progress = []
        anchor = lambda: _zero_from(progress[-1], x_ref.shape[1]) if progress else 0.0
        _interleave([_tail_stage(y_ref, x_ref[...], ocv_buf[...], p_ref[...], *tail_refs, ff_chunk, anchor),
                     mix(progress)], [2, 4])


def _out_sample_kernel(ff_chunk, x_ref, p_ref, o_ref, z_ref, glu_ref, wdw_ref, go_ref, bdw_ref, gcln_ref, bcln_ref,
                       hc_ref, *rest):
    *tail_refs, y_ref, nc_ref = rest
    n_tok = glu_ref.shape[0]
    n_dw = wdw_ref.shape[0]
    full_c = [hc_ref[j] for j in range(n_dw - 1)] + [glu_ref[t] for t in range(n_tok)]
    for j in range(n_dw - 1):
        nc_ref[j] = full_c[n_tok + j]
    rows = []
    for t in range(n_tok):
        acc = full_c[t] * wdw_ref[0:1]
        for j in range(1, n_dw):
            acc = acc + full_c[t + j] * wdw_ref[j:j + 1]
        rows.append(acc)
    cat = lambda ref: jnp.concatenate([ref[t] for t in range(n_tok)], axis=0)
    ones_bd = _block_ones(GROUP_W)
    ocv = jnp.concatenate([_o_norm(cat(o_ref), cat(z_ref), go_ref, ones_bd),
                           _conv_norm(jnp.concatenate(rows, axis=0), bdw_ref, gcln_ref, bcln_ref, ones_bd)],
                          axis=1).astype(BF16)
    _drain(_tail_stage(y_ref, x_ref[...], ocv, p_ref[...], *tail_refs, ff_chunk))


def _out_call(kernel_fn, name, x, behind, ahead, consts, tm, lag, extra_out_shape, extra_out_spec, scratch):
    t, d = x.shape
    nt = t // tm
    late = lambda i: jnp.maximum(i - lag, 0)
    early = lambda i: jnp.minimum(i, nt - 1)
    whole = lambda a, **kw: pl.BlockSpec(a.shape, lambda i: (0,) * a.ndim, **kw)
    tiled = lambda a, tile_of: (pl.BlockSpec((tm, a.shape[1]), lambda i: (tile_of(i), 0)) if a.ndim == 2
                                else whole(a))
    return pl.pallas_call(
        kernel_fn,
        grid=(nt + lag,),
        in_specs=[tiled(a, late) for a in (x, *behind)] + [tiled(a, early) for a in ahead]
        + [whole(a, pipeline_mode=pl.Buffered(1)) for a in consts],
        out_specs=[tiled(x, late), extra_out_spec],
        out_shape=[jax.ShapeDtypeStruct((t, d), F32), extra_out_shape],
        scratch_shapes=scratch,
        compiler_params=pltpu.CompilerParams(dimension_semantics=("arbitrary",), vmem_limit_bytes=VMEM_LIMIT),
        name=name,
    )(x, *behind, *ahead, *consts)


def kernel(x_prompt, x_sample, state_ssm, state_qkv_conv, state_conf_conv, p_prompt, p_sample, g_mix, w_in, w_short,
           a_log, dt_bias, g_o, w_dw, b_dw, g_cln, b_cln, w_out, g_mlp, w_up, w_down, w_ple, g_ple, w_ple_gate,
           g_final):
    depth = w_in.shape[0]
    assert depth == 1, "single-layer trunk"
    bp, seq, d = x_prompt.shape
    bs, n_tok, _ = x_sample.shape
    n_heads = a_log.shape[1]
    dn = n_heads * HEAD_DIM
    n_dw, cc = w_dw.shape[1], w_dw.shape[2]
    assert dn % GROUP_W == 0 and cc % GROUP_W == 0 and CHUNK % n_tok == 0
    tm_in, tm_out, ff_chunk = 512, 512, 1024
    assert seq % tm_out == 0 and seq % CHUNK == 0 and (bp * seq) % tm_in == 0

    wt = jnp.swapaxes(w_in[0], 0, 1).astype(BF16)
    gp = jnp.zeros((2, 2 * n_heads), F32).at[0, n_heads:].set(a_log[0]).at[1, n_heads:].set(dt_bias[0])
    row = lambda a: a.reshape(1, -1)
    norms = (row(jnp.tile(g_o[0], n_heads)), row(b_dw[0]), row(g_cln[0]), row(b_cln[0]))
    tail = (w_out[0].astype(BF16), row(g_mlp[0]), w_up[0].astype(BF16), w_down[0].astype(BF16), row(g_ple[0]),
            w_ple_gate[0].astype(BF16), w_ple[0].astype(BF16), row(g_final))

    xp = x_prompt.reshape(bp * seq, d)
    qkv, z, ba, glu = _inproj(xp, row(g_mix[0]), wt, gp, n_heads, dn, cc, tm=tm_in)
    o_raw, ssm_p, qkv_p = _seq_prompt(qkv, ba, w_short[0], batch=bp)
    cpad = -(-(n_dw - 1) // SUBLANES) * SUBLANES
    nt_out, tiles_per_seq = bp * seq // tm_out, seq // tm_out
    y_prompt, conf_p = _out_call(
        functools.partial(_out_prompt_kernel, tm_out, tiles_per_seq, ff_chunk), "out_prompt", xp,
        [p_prompt[0].reshape(bp * seq, -1)], [o_raw, z, glu], (w_dw[0],) + norms + tail, tm_out, 1,
        jax.ShapeDtypeStruct((bp, n_dw - 1, cc), F32),
        pl.BlockSpec((1, n_dw - 1, cc), lambda i: (jnp.minimum(i, nt_out - 1) // tiles_per_seq, 0, 0)),
        [pltpu.VMEM((cpad + tm_out + SUBLANES, cc), F32), pltpu.VMEM((tm_out, dn + cc), BF16),
         pltpu.VMEM((GROUP_W, GROUP_W), BF16)])

    xs = jnp.swapaxes(x_sample, 0, 1).reshape(n_tok * bs, d)
    qkv, z, ba, glu = _inproj(xs, row(g_mix[0]), wt, gp, n_heads, dn, cc, tm=n_tok * bs)
    tmaj = lambda a: a.reshape(n_tok, bs, -1)
    o_raw, ssm_s, qkv_s = _seq_sample(jnp.swapaxes(state_qkv_conv[0], 0, 1), tmaj(qkv), tmaj(ba), state_ssm[0],
                                      w_short[0])
    ps = jnp.swapaxes(p_sample[0], 0, 1).reshape(n_tok * bs, -1)
    hc = jnp.swapaxes(state_conf_conv[0], 0, 1)
    ys, conf_s = _out_call(
        functools.partial(_out_sample_kernel, ff_chunk), "out_sample", xs,
        [ps], [o_raw, tmaj(z), tmaj(glu)], (w_dw[0],) + norms + (hc,) + tail, n_tok * bs, 0,
        jax.ShapeDtypeStruct(hc.shape, F32), pl.BlockSpec(hc.shape, lambda i: (0, 0, 0)), [])
    y_sample = jnp.swapaxes(ys.reshape(n_tok, bs, d), 0, 1)

    return (y_prompt.reshape(bp, seq, d), y_sample, ssm_p[None], qkv_p[None], conf_p[None], ssm_s[None],
            jnp.swapaxes(qkv_s, 0, 1)[None], jnp.swapaxes(conf_s, 0, 1)[None])
```

```python
import functools

import jax
import jax.numpy as jnp
from jax import lax
from jax.experimental import pallas as pl
from jax.experimental.pallas import tpu as pltpu

F32 = jnp.float32
BF16 = jnp.bfloat16
EPS = 1e-6
HEAD_DIM = 64
CHUNK = 64
GROUP_W = 256
HEADS_PER_GROUP = GROUP_W // HEAD_DIM
SUBLANES = 8
VMEM_LIMIT = 56 * 1024 * 1024


def _dot(a, b):
    return jnp.dot(a.astype(BF16), b.astype(BF16), preferred_element_type=F32)


def _dot_nt(a, b):
    return lax.dot_general(a.astype(BF16), b.astype(BF16), (((1,), (1,)), ((), ())), preferred_element_type=F32)


def _sigmoid(x):
    return 1.0 / (1.0 + jnp.exp(-x))


def _silu(x):
    return x * _sigmoid(x)


def _softplus(x):
    return jnp.maximum(x, 0.0) + jnp.log1p(jnp.exp(-jnp.abs(x)))


def _rms(x, gain):
    return x * lax.rsqrt(jnp.mean(x * x, axis=-1, keepdims=True) + EPS) * gain


def _split3(x):
    hi = x.astype(BF16)
    r = x - hi.astype(F32)
    mid = r.astype(BF16)
    lo = (r - mid.astype(F32)).astype(BF16)
    return hi, mid, lo


def _dot_exact(a01, x):
    hi, mid, lo = _split3(x)
    d = lambda p: jnp.dot(a01, p, preferred_element_type=F32)
    return d(hi) + (d(mid) + d(lo))


def _block_ones(n):
    shift = HEAD_DIM.bit_length() - 1
    r = lax.broadcasted_iota(jnp.int32, (n, n), 0) >> shift
    c = lax.broadcasted_iota(jnp.int32, (n, n), 1) >> shift
    return (r == c).astype(BF16)


def _group_sum(x, ones_bd):
    outs = [jnp.dot(x[:, GROUP_W * b:GROUP_W * (b + 1)].astype(BF16), ones_bd, preferred_element_type=F32)
            for b in range(x.shape[1] // GROUP_W)]
    return jnp.concatenate(outs, axis=1)


def _expand_heads(cols, off, n_heads):
    rows = cols.shape[0]
    lo_half = lax.broadcasted_iota(jnp.int32, (rows, 128), 1) < HEAD_DIM
    blocks = []
    for b in range(n_heads // 2):
        c0 = cols[:, off + 2 * b:off + 2 * b + 1]
        c1 = cols[:, off + 2 * b + 1:off + 2 * b + 2]
        blocks.append(jnp.where(lo_half, c0, c1))
    return jnp.concatenate(blocks, axis=1)


class _Masks:
    SHAPES = dict(strict=((CHUNK, GROUP_W), F32), causal=((CHUNK, GROUP_W), F32), eye_cat=((CHUNK, GROUP_W), F32),
                  bd=((GROUP_W, GROUP_W), BF16), bd_f32=((GROUP_W, GROUP_W), F32), tri=((CHUNK, CHUNK), BF16),
                  ones_cc=((CHUNK, CHUNK), BF16))

    @classmethod
    def scratch(cls):
        return [pltpu.VMEM(shape, dtype) for shape, dtype in cls.SHAPES.values()]

    def __init__(self, refs, pool_ref=None):
        self._refs = dict(zip(self.SHAPES, refs))
        self.pool = _BlockDiagPool(pool_ref)

    def __getattr__(self, name):
        return self._refs[name][...]

    def fill(self, seqs):
        c = CHUNK
        r = self._refs
        ri = lax.broadcasted_iota(jnp.int32, (c, GROUP_W), 0)
        lj = lax.broadcasted_iota(jnp.int32, (c, GROUP_W), 1) & (HEAD_DIM - 1)
        same = (ri & (seqs - 1)) == (lj & (seqs - 1))
        r["strict"][...] = (same & (ri > lj)).astype(F32)
        r["causal"][...] = (same & (ri >= lj)).astype(F32)
        r["eye_cat"][...] = (ri == lj).astype(F32)
        r["bd"][...] = _block_ones(GROUP_W)
        r["bd_f32"][...] = _block_ones(GROUP_W).astype(F32)
        r3 = lax.broadcasted_iota(jnp.int32, (c, c), 0)
        c3 = lax.broadcasted_iota(jnp.int32, (c, c), 1)
        r["tri"][...] = (((r3 & (seqs - 1)) == (c3 & (seqs - 1))) & (r3 >= c3)).astype(BF16)
        r["ones_cc"][...] = jnp.ones((c, c), BF16)


class _BlockDiagPool:
    def __init__(self, ref):
        self.ref = ref
        self.used = 0

    def zero(self):
        self.ref[...] = jnp.zeros(self.ref.shape, self.ref.dtype)

    def expand(self, y):
        n = self.used
        self.used += 1
        yb = y.astype(BF16)
        for h in range(HEADS_PER_GROUP):
            blk = slice(HEAD_DIM * h, HEAD_DIM * (h + 1))
            self.ref[n, blk, blk] = yb[:, blk]
        return self.ref[n]


def _block_diag(y, m):
    return m.pool.expand(y)


def _interleave(gens, weights):
    live = dict(zip(gens, weights))
    while live:
        for g, w in tuple(live.items()):
            for _ in range(w):
                try:
                    next(g)
                except StopIteration:
                    live.pop(g, None)
                    break


def _zero_from(v, width):
    bits = lax.bitcast_convert_type(v[0:1], jnp.int32)
    half = jnp.full(bits.shape, 16, jnp.int32)
    z = lax.shift_right_logical(lax.shift_right_logical(bits, half), half).astype(F32)
    return jnp.concatenate([z] * (width // z.shape[1]), axis=1)


def _drain(gen):
    for _ in gen:
        pass


def _prep_stage(out, acc, ba, g_last_of, dn, n_heads, m):
    qkv = _silu(acc)
    yield
    q, k, v = qkv[:, :dn], qkv[:, dn:2 * dn], qkv[:, 2 * dn:]
    qs = q * (lax.rsqrt(_group_sum(q * q, m.bd) + EPS) * (HEAD_DIM ** -0.5))
    k = k * lax.rsqrt(_group_sum(k * k, m.bd) + EPS)
    yield
    gcs = _dot_exact(m.tri, ba)
    bexp = _expand_heads(ba, 0, n_heads)
    g_row = _expand_heads(gcs, n_heads, n_heads)
    g_last = g_last_of(g_row)
    yield
    egc = jnp.exp(g_row)
    kb = k * bexp
    vb = v * bexp
    kbd = kb * egc
    qd = qs * egc
    kd = k * jnp.exp(g_last - g_row)
    yield
    for grp in range(dn // GROUP_W):
        sl = slice(GROUP_W * grp, GROUP_W * (grp + 1))
        gr = g_row[:, sl]
        g_col = _dot_exact(m.ones_cc, gr * m.eye_cat)
        dec = jnp.exp(jnp.minimum(gr - g_col, 0.0)) * m.causal
        aq = _dot_nt(jnp.concatenate([kb[:, sl], qs[:, sl]], axis=0), _block_diag(k[:, sl], m))
        out.append(dict(a=aq[:CHUNK] * (dec * m.strict), qk=aq[CHUNK:] * dec,
                        rhs=jnp.concatenate([_block_diag(vb[:, sl], m), _block_diag(kbd[:, sl], m)], axis=1),
                        qd=qd[:, sl], kd=kd[:, sl], g_last=g_last[:, sl]))
        yield


def _solve_stage(probs, m, levels):
    pws = [-p["a"] for p in probs]
    ts = [m.eye_cat + pw for pw in pws]
    for _ in range(levels):
        pws = [_dot(pw, _block_diag(pw, m)) for pw in pws]
        ts = [t + _dot(t, _block_diag(pw, m)) for t, pw in zip(ts, pws)]
        yield
    for p, t in zip(probs, ts):
        uw = _dot(t, p["rhs"])
        p["u"], p["w"] = uw[:, :GROUP_W], uw[:, GROUP_W:]
    yield


def _inproj_kernel(n_heads, dn, cc, x_ref, gmix_ref, wt_ref, gp_ref, qkv_ref, z_ref, ba_ref, glu_ref):
    hb = _rms(x_ref[...], gmix_ref[...]).astype(BF16)
    d = lambda r0, r1: lax.dot_general(hb, wt_ref[r0:r1], (((1,), (1,)), ((), ())), preferred_element_type=F32)
    o_z = 3 * dn
    o_ba = o_z + dn
    o_ga = o_ba + 2 * n_heads
    o_gb = o_ga + cc
    qkv_ref[...] = d(0, o_z)
    z_ref[...] = _silu(d(o_z, o_ba))
    glu_ref[...] = d(o_ga, o_gb) * _sigmoid(d(o_gb, o_gb + cc))
    ba = d(o_ba, o_ga)
    col = lax.broadcasted_iota(jnp.int32, ba.shape, 1)
    g = -jnp.exp(gp_ref[0:1, :]) * _softplus(ba + gp_ref[1:2, :])
    ba_ref[...] = jnp.where(col < n_heads, _sigmoid(ba), g)


def _inproj(x, gmix, wt, gp, n_heads, dn, cc, tm):
    t, d = x.shape
    const = lambda shape: pl.BlockSpec(shape, lambda i: (0, 0), pipeline_mode=pl.Buffered(1))
    row = lambda w: pl.BlockSpec((tm, w), lambda i: (i, 0))
    return pl.pallas_call(
        functools.partial(_inproj_kernel, n_heads, dn, cc),
        grid=(t // tm,),
        in_specs=[row(d), const(gmix.shape), const(wt.shape), const(gp.shape)],
        out_specs=[row(3 * dn), row(dn), row(2 * n_heads), row(cc)],
        out_shape=[jax.ShapeDtypeStruct((t, 3 * dn), F32), jax.ShapeDtypeStruct((t, dn), F32),
                   jax.ShapeDtypeStruct((t, 2 * n_heads), F32), jax.ShapeDtypeStruct((t, cc), F32)],
        compiler_params=pltpu.CompilerParams(dimension_semantics=("arbitrary",), vmem_limit_bytes=VMEM_LIMIT),
        name="inproj",
    )(x, gmix, wt, gp)


def _seq_prompt_kernel(dn, n_short, qkv_ref, ba_ref, wshort_ref, o_ref, sout_ref, qh_ref, hist, sbd, pool_ref,
                       *mask_refs):
    step = pl.program_id(0)
    n_seq = qkv_ref.shape[0]
    xpad = hist.shape[1]
    n_heads = dn // HEAD_DIM
    n_groups = dn // GROUP_W
    m = _Masks(mask_refs, pool_ref)
    levels = (CHUNK - 1).bit_length() - 1

    @pl.when(step == 0)
    def _():
        m.fill(1)
        m.pool.zero()
        hist[...] = jnp.zeros(hist.shape, F32)
        sbd[...] = jnp.zeros(sbd.shape, F32)

    probs = [[] for _ in range(n_seq)]

    def prep(b):
        window = jnp.concatenate([hist[b], qkv_ref[b]], axis=0)
        acc = window * wshort_ref[0:1]
        for j in range(1, n_short):
            acc = pltpu.roll(acc, 1, 0) + window * wshort_ref[j:j + 1]
        yield
        yield from _prep_stage(probs[b], acc[xpad:], ba_ref[b], lambda g_row: g_row[CHUNK - 1:CHUNK], dn, n_heads, m)

    def recur(b):
        o_parts = []
        for grp, p in enumerate(probs[b]):
            s = sbd[n_groups * b + grp]
            wq = _dot(jnp.concatenate([p["w"], p["qd"]], axis=0), s)
            v_new = p["u"] - wq[:CHUNK]
            yield
            o_parts.append(wq[CHUNK:] + _dot(p["qk"], _block_diag(v_new, m)))
            sbd[n_groups * b + grp] = s * jnp.exp(p["g_last"]) + _dot(p["kd"].T, v_new) * m.bd_f32
            yield
        o_ref[b] = jnp.concatenate(o_parts, axis=1)

    for b in range(n_seq):
        _drain(prep(b))
    _drain(_solve_stage([p for pb in probs for p in pb], m, levels))
    _interleave([recur(b) for b in range(n_seq)], [1] * n_seq)

    for b in range(n_seq):
        tail = qkv_ref[b, CHUNK - xpad:CHUNK]
        hist[b] = tail
        qh_ref[b] = tail[xpad - (n_short - 1):]

    @pl.when(step == pl.num_programs(0) - 1)
    def _():
        for b in range(n_seq):
            for h in range(n_heads):
                grp, hl = divmod(h, HEADS_PER_GROUP)
                sout_ref[b, h] = sbd[n_groups * b + grp, HEAD_DIM * hl:HEAD_DIM * (hl + 1),
                                     HEAD_DIM * hl:HEAD_DIM * (hl + 1)]


def _bd_uses(n_chunks, n_groups, levels=(CHUNK - 1).bit_length() - 1):
    return n_chunks * n_groups * (1 + 2 + 2 * levels + 1)


def _seq_prompt(qkv, ba, wshort, batch):
    t, dn3 = qkv.shape
    dn = dn3 // 3
    seq = t // batch
    n_heads = dn // HEAD_DIM
    n_groups = dn // GROUP_W
    n_short = wshort.shape[0]
    xpad = -(-(n_short - 1) // SUBLANES) * SUBLANES
    chunk_of = lambda w: pl.BlockSpec((batch, CHUNK, w), lambda c: (0, c, 0))
    whole = lambda shape: pl.BlockSpec(shape, lambda c: (0,) * len(shape))
    o, sout, qh = pl.pallas_call(
        functools.partial(_seq_prompt_kernel, dn, n_short),
        grid=(seq // CHUNK,),
        in_specs=[chunk_of(dn3), chunk_of(ba.shape[1]), whole(wshort.shape)],
        out_specs=[chunk_of(dn), whole((batch, n_heads, HEAD_DIM, HEAD_DIM)), whole((batch, n_short - 1, dn3))],
        out_shape=[jax.ShapeDtypeStruct((batch, seq, dn), F32),
                   jax.ShapeDtypeStruct((batch, n_heads, HEAD_DIM, HEAD_DIM), F32),
                   jax.ShapeDtypeStruct((batch, n_short - 1, dn3), F32)],
        scratch_shapes=[pltpu.VMEM((batch, xpad, dn3), F32), pltpu.VMEM((batch * n_groups, GROUP_W, GROUP_W), F32),
                        pltpu.VMEM((_bd_uses(batch, n_groups), GROUP_W, GROUP_W), BF16)] + _Masks.scratch(),
        compiler_params=pltpu.CompilerParams(dimension_semantics=("arbitrary",), vmem_limit_bytes=VMEM_LIMIT),
        name="seq_prompt",
    )(qkv.reshape(batch, seq, dn3), ba.reshape(batch, seq, ba.shape[1]), wshort)
    return o.reshape(t, dn), sout, qh


def _seq_sample_kernel(dn, hq_ref, xq_ref, ba_ref, s0_ref, wshort_ref, o_ref, sout_ref, nq_ref, pool_ref, *mask_refs):
    n_tok, seqs = xq_ref.shape[0], xq_ref.shape[1]
    n_heads = dn // HEAD_DIM
    n_short = wshort_ref.shape[0]
    m = _Masks(mask_refs, pool_ref)
    levels = max((n_tok - 1).bit_length() - 1, 0)

    @pl.when(pl.program_id(0) == 0)
    def _():
        m.fill(seqs)
        m.pool.zero()

    full_q = [hq_ref[j] for j in range(n_short - 1)] + [xq_ref[t] for t in range(n_tok)]
    for j in range(n_short - 1):
        nq_ref[j] = full_q[n_tok + j]
    rows = []
    for t in range(n_tok):
        acc = full_q[t] * wshort_ref[0:1]
        for j in range(1, n_short):
            acc = acc + full_q[t + j] * wshort_ref[j:j + 1]
        rows.append(acc)
    ba = jnp.concatenate([ba_ref[t] for t in range(n_tok)], axis=0)
    probs = []
    g_last_of = lambda g_row: jnp.concatenate([g_row[(n_tok - 1) * seqs:]] * n_tok, axis=0)
    _drain(_prep_stage(probs, jnp.concatenate(rows, axis=0), ba, g_last_of, dn, n_heads, m))
    _drain(_solve_stage(probs, m, levels))

    row_seq = lax.broadcasted_iota(jnp.int32, (2 * CHUNK, GROUP_W), 0) & (seqs - 1)
    col_seq = lax.broadcasted_iota(jnp.int32, (GROUP_W, CHUNK), 1) & (seqs - 1)
    o_parts = []
    for grp, p in enumerate(probs):
        lhs = jnp.concatenate([p["w"], p["qd"]], axis=0).astype(BF16)
        s_bd = []
        wq = jnp.zeros((2 * CHUNK, GROUP_W), F32)
        for s in range(seqs):
            s_rows = s0_ref[s, HEADS_PER_GROUP * grp:HEADS_PER_GROUP * (grp + 1)].reshape(GROUP_W, HEAD_DIM)
            s_full = jnp.concatenate([s_rows] * HEADS_PER_GROUP, axis=1) * m.bd_f32
            s_bd.append(s_full)
            wq = jnp.where(row_seq == s, _dot(lhs, s_full), wq)
        v_new = p["u"] - wq[:CHUNK]
        o_parts.append(wq[CHUNK:] + _dot(p["qk"], _block_diag(v_new, m)))
        kd_t = p["kd"].T.astype(BF16)
        for s in range(seqs):
            upd = _dot(jnp.where(col_seq == s, kd_t, jnp.zeros((), BF16)), v_new)
            s_new = s_bd[s] * jnp.exp(p["g_last"][s:s + 1]) + upd * m.bd_f32
            for hl in range(HEADS_PER_GROUP):
                sout_ref[s, HEADS_PER_GROUP * grp + hl] = s_new[HEAD_DIM * hl:HEAD_DIM * (hl + 1),
                                                                 HEAD_DIM * hl:HEAD_DIM * (hl + 1)]
    o = jnp.concatenate(o_parts, axis=1)
    for t in range(n_tok):
        o_ref[t] = o[t * seqs:(t + 1) * seqs]


def _seq_sample(hq, xq, ba, s0, wshort):
    n_tok, nseq, dn3 = xq.shape
    dn = dn3 // 3
    seqs = CHUNK // n_tok
    n_heads = dn // HEAD_DIM
    tm = lambda a: pl.BlockSpec((a.shape[0], seqs, a.shape[2]), lambda i: (0, i, 0))
    st = pl.BlockSpec((seqs, n_heads, HEAD_DIM, HEAD_DIM), lambda i: (i, 0, 0, 0))
    o_shape = jax.ShapeDtypeStruct((n_tok, nseq, dn), F32)
    return pl.pallas_call(
        functools.partial(_seq_sample_kernel, dn),
        grid=(nseq // seqs,),
        in_specs=[tm(hq), tm(xq), tm(ba), st, pl.BlockSpec(wshort.shape, lambda i: (0, 0))],
        out_specs=[tm(o_shape), st, tm(hq)],
        out_shape=[o_shape, jax.ShapeDtypeStruct(s0.shape, F32), jax.ShapeDtypeStruct(hq.shape, F32)],
        scratch_shapes=[pltpu.VMEM((_bd_uses(1, dn // GROUP_W, max((n_tok - 1).bit_length() - 1, 0)), GROUP_W, GROUP_W), BF16)]
        + _Masks.scratch(),
        compiler_params=pltpu.CompilerParams(dimension_semantics=("arbitrary",), vmem_limit_bytes=VMEM_LIMIT),
        name="seq_sample",
    )(hq, xq, ba, s0, wshort)


def _o_norm(o_raw, z_silu, go_ref, ones_bd):
    ms = _group_sum(o_raw * o_raw, ones_bd) * (1.0 / HEAD_DIM)
    return o_raw * lax.rsqrt(ms + EPS) * go_ref[...] * z_silu


def _conv_norm(conv, bdw_ref, gcln_ref, bcln_ref, ones_bd):
    inv = 1.0 / HEAD_DIM
    cv = conv + bdw_ref[...]
    d = cv - _group_sum(cv, ones_bd) * inv
    return _silu(d * lax.rsqrt(_group_sum(d * d, ones_bd) * inv + EPS) * gcln_ref[...] + bcln_ref[...])


def _fold_rows(x, prev=None):
    acc = x[0:SUBLANES] if prev is None else x[0:SUBLANES] + prev
    for r in range(SUBLANES, x.shape[0], SUBLANES):
        acc = acc + x[r:r + SUBLANES]
    return acc


def _tail_stage(y_ref, x, ocv, p, wout_ref, gmlp_ref, wup_ref, wdown_ref, gple_ref, wgate_ref, wple_ref, gfin_ref,
                ff_chunk, anchor=lambda: 0.0):
    def dot_cols(a, w_ref, r0=0, r1=None, c0=0, c1=None):
        a = a.astype(BF16)
        r1 = w_ref.shape[0] if r1 is None else r1
        c1 = w_ref.shape[1] if c1 is None else c1
        cols = []
        for c in range(c0, c1, GROUP_W):
            cols.append(jnp.dot(a, w_ref[r0:r1, c:c + GROUP_W], preferred_element_type=F32))
            yield
        return jnp.concatenate(cols, axis=1)

    x = x + (yield from dot_cols(ocv, wout_ref))
    hb = _rms(x, gmlp_ref[...] + anchor()).astype(BF16)
    mlp = None
    for f in range(wup_ref.shape[1] // ff_chunk):
        up = jnp.maximum((yield from dot_cols(hb, wup_ref, c0=f * ff_chunk, c1=(f + 1) * ff_chunk)), anchor())
        down = yield from dot_cols(up * up, wdown_ref, r0=f * ff_chunk, r1=(f + 1) * ff_chunk)
        mlp = down if mlp is None else mlp + down
    x = x + mlp
    gate = _sigmoid((yield from dot_cols(_rms(x, gple_ref[...] + anchor()), wgate_ref)))
    x = x + (yield from dot_cols(p, wple_ref)) * gate
    y_ref[...] = _rms(x, gfin_ref[...] + anchor())


def _mix_stage(progress, ocv_buf, nc_ref, o_ref, z_ref, glu_ref, wdw_ref, norm_refs, ones_ref, cbuf, tm, block):
    n_dw = wdw_ref.shape[0]
    cpad = cbuf.shape[0] - tm - SUBLANES
    first = cpad - (n_dw - 1)
    go_ref, bdw_ref, gcln_ref, bcln_ref = norm_refs
    cbuf[cpad:cpad + tm] = glu_ref[...]
    o = _o_norm(o_ref[...], z_ref[...], go_ref, ones_ref[...])
    progress.append(_fold_rows(o))
    yield
    blocks = []
    for r0 in range(0, tm, block):
        conv = None
        for s in range(SUBLANES):
            part = None
            for a in range((first + n_dw - 1) // SUBLANES + 1):
                j = SUBLANES * a + s - first
                if 0 <= j < n_dw:
                    lo = SUBLANES * a + r0
                    term = cbuf[lo:lo + block + SUBLANES] * wdw_ref[j:j + 1]
                    part = term if part is None else part + term
            if part is not None:
                conv = part[s:s + block] if conv is None else conv + part[s:s + block]
            yield
        blocks.append(conv)
        progress.append(_fold_rows(conv, progress[-1]))
    cbuf[0:cpad] = cbuf[tm:tm + cpad]
    nc_ref[0] = cbuf[cpad - (n_dw - 1):cpad]
    cv = _conv_norm(jnp.concatenate(blocks, axis=0), bdw_ref, gcln_ref, bcln_ref, ones_ref[...])
    progress.append(_fold_rows(cv, progress[-1]))
    ocv_buf[...] = jnp.concatenate([o, cv], axis=1).astype(BF16)


def _out_prompt_kernel(tm, tiles_per_seq, ff_chunk, x_ref, p_ref, o_ref, z_ref, glu_ref, wdw_ref, go_ref, bdw_ref,
                       gcln_ref, bcln_ref, *rest):
    *tail_refs, y_ref, nc_ref, cbuf, ocv_buf, ones_ref = rest
    step = pl.program_id(0)
    cpad = cbuf.shape[0] - tm - SUBLANES

    @pl.when(step == 0)
    def _():
        ones_ref[...] = _block_ones(GROUP_W)
        cbuf[cpad + tm:] = jnp.zeros((SUBLANES, cbuf.shape[1]), F32)

    @pl.when(step % tiles_per_seq == 0)
    def _():
        cbuf[0:cpad] = jnp.zeros((cpad, cbuf.shape[1]), F32)

    mix = lambda progress: _mix_stage(progress, ocv_buf, nc_ref, o_ref, z_ref, glu_ref, wdw_ref,
                                      (go_ref, bdw_ref, gcln_ref, bcln_ref), ones_ref, cbuf, tm, CHUNK)

    @pl.when(step == 0)
    def _():
        _drain(mix([]))

    @pl.when(step > 0)
    def _():
        progress = []
        anchor = lambda: _zero_from(progress[-1], x_ref.shape[1]) if progress else 0.0
        _interleave([_tail_stage(y_ref, x_ref[...], ocv_buf[...], p_ref[...], *tail_refs, ff_chunk, anchor),
                     mix(progress)], [2, 4])


def _out_sample_kernel(ff_chunk, x_ref, p_ref, o_ref, z_ref, glu_ref, wdw_ref, go_ref, bdw_ref, gcln_ref, bcln_ref,
                       hc_ref, *rest):
    *tail_refs, y_ref, nc_ref = rest
    n_tok = glu_ref.shape[0]
    n_dw = wdw_ref.shape[0]
    full_c = [hc_ref[j] for j in range(n_dw - 1)] + [glu_ref[t] for t in range(n_tok)]
    for j in range(n_dw - 1):
        nc_ref[j] = full_c[n_tok + j]
    rows = []
    for t in range(n_tok):
        acc = full_c[t] * wdw_ref[0:1]
        for j in range(1, n_dw):
            acc = acc + full_c[t + j] * wdw_ref[j:j + 1]
        rows.append(acc)
    cat = lambda ref: jnp.concatenate([ref[t] for t in range(n_tok)], axis=0)
    ones_bd = _block_ones(GROUP_W)
    ocv = jnp.concatenate([_o_norm(cat(o_ref), cat(z_ref), go_ref, ones_bd),
                           _conv_norm(jnp.concatenate(rows, axis=0), bdw_ref, gcln_ref, bcln_ref, ones_bd)],
                          axis=1).astype(BF16)
    _drain(_tail_stage(y_ref, x_ref[...], ocv, p_ref[...], *tail_refs, ff_chunk))


def _out_call(kernel_fn, name, x, behind, ahead, consts, tm, lag, extra_out_shape, extra_out_spec, scratch):
    t, d = x.shape
    nt = t // tm
    late = lambda i: jnp.maximum(i - lag, 0)
    early = lambda i: jnp.minimum(i, nt - 1)
    whole = lambda a, **kw: pl.BlockSpec(a.shape, lambda i: (0,) * a.ndim, **kw)
    tiled = lambda a, tile_of: (pl.BlockSpec((tm, a.shape[1]), lambda i: (tile_of(i), 0)) if a.ndim == 2
                                else whole(a))
    return pl.pallas_call(
        kernel_fn,
        grid=(nt + lag,),
        in_specs=[tiled(a, late) for a in (x, *behind)] + [tiled(a, early) for a in ahead]
        + [whole(a, pipeline_mode=pl.Buffered(1)) for a in consts],
        out_specs=[tiled(x, late), extra_out_spec],
        out_shape=[jax.ShapeDtypeStruct((t, d), F32), extra_out_shape],
        scratch_shapes=scratch,
        compiler_params=pltpu.CompilerParams(dimension_semantics=("arbitrary",), vmem_limit_bytes=VMEM_LIMIT),
        name=name,
    )(x, *behind, *ahead, *consts)


def _transpose_kernel(x_ref, o_ref):
    o_ref[...] = x_ref[...].T


def _batch_major(state, block_rows=4096):
    b = state.shape[0]
    flat = jnp.moveaxis(state, 0, -1).reshape(-1, b)
    rows = flat.shape[0]
    assert rows % block_rows == 0
    out = pl.pallas_call(
        _transpose_kernel,
        grid=(rows // block_rows,),
        in_specs=[pl.BlockSpec((block_rows, b), lambda i: (i, 0))],
        out_specs=pl.BlockSpec((b, block_rows), lambda i: (0, i)),
        out_shape=jax.ShapeDtypeStruct((b, rows), state.dtype),
        compiler_params=pltpu.CompilerParams(dimension_semantics=("arbitrary",), vmem_limit_bytes=VMEM_LIMIT),
        name="state_transpose",
    )(flat)
    return out.reshape(state.shape)


def kernel(x_prompt, x_sample, state_ssm, state_qkv_conv, state_conf_conv, p_prompt, p_sample, g_mix, w_in, w_short,
           a_log, dt_bias, g_o, w_dw, b_dw, g_cln, b_cln, w_out, g_mlp, w_up, w_down, w_ple, g_ple, w_ple_gate,
           g_final):
    depth = w_in.shape[0]
    assert depth == 1, "single-layer trunk"
    bp, seq, d = x_prompt.shape
    bs, n_tok, _ = x_sample.shape
    n_heads = a_log.shape[1]
    dn = n_heads * HEAD_DIM
    n_dw, cc = w_dw.shape[1], w_dw.shape[2]
    assert dn % GROUP_W == 0 and cc % GROUP_W == 0 and CHUNK % n_tok == 0
    tm_in, tm_out, ff_chunk = 512, 512, 1024
    assert seq % tm_out == 0 and seq % CHUNK == 0 and (bp * seq) % tm_in == 0

    wt = jnp.swapaxes(w_in[0], 0, 1).astype(BF16)
    gp = jnp.zeros((2, 2 * n_heads), F32).at[0, n_heads:].set(a_log[0]).at[1, n_heads:].set(dt_bias[0])
    row = lambda a: a.reshape(1, -1)
    norms = (row(jnp.tile(g_o[0], n_heads)), row(b_dw[0]), row(g_cln[0]), row(b_cln[0]))
    tail = (w_out[0].astype(BF16), row(g_mlp[0]), w_up[0].astype(BF16), w_down[0].astype(BF16), row(g_ple[0]),
            w_ple_gate[0].astype(BF16), w_ple[0].astype(BF16), row(g_final))

    xp = x_prompt.reshape(bp * seq, d)
    qkv, z, ba, glu = _inproj(xp, row(g_mix[0]), wt, gp, n_heads, dn, cc, tm=tm_in)
    o_raw, ssm_p, qkv_p = _seq_prompt(qkv, ba, w_short[0], batch=bp)
    cpad = -(-(n_dw - 1) // SUBLANES) * SUBLANES
    nt_out, tiles_per_seq = bp * seq // tm_out, seq // tm_out
    y_prompt, conf_p = _out_call(
        functools.partial(_out_prompt_kernel, tm_out, tiles_per_seq, ff_chunk), "out_prompt", xp,
        [p_prompt[0].reshape(bp * seq, -1)], [o_raw, z, glu], (w_dw[0],) + norms + tail, tm_out, 1,
        jax.ShapeDtypeStruct((bp, n_dw - 1, cc), F32),
        pl.BlockSpec((1, n_dw - 1, cc), lambda i: (jnp.minimum(i, nt_out - 1) // tiles_per_seq, 0, 0)),
        [pltpu.VMEM((cpad + tm_out + SUBLANES, cc), F32), pltpu.VMEM((tm_out, dn + cc), BF16),
         pltpu.VMEM((GROUP_W, GROUP_W), BF16)])

    xs = jnp.swapaxes(x_sample, 0, 1).reshape(n_tok * bs, d)
    qkv, z, ba, glu = _inproj(xs, row(g_mix[0]), wt, gp, n_heads, dn, cc, tm=n_tok * bs)
    tmaj = lambda a: a.reshape(n_tok, bs, -1)
    o_raw, ssm_s, qkv_s = _seq_sample(jnp.swapaxes(state_qkv_conv[0], 0, 1), tmaj(qkv), tmaj(ba),
                                      _batch_major(state_ssm[0]),
                                      w_short[0])
    ps = jnp.swapaxes(p_sample[0], 0, 1).reshape(n_tok * bs, -1)
    hc = jnp.swapaxes(state_conf_conv[0], 0, 1)
    ys, conf_s = _out_call(
        functools.partial(_out_sample_kernel, ff_chunk), "out_sample", xs,
        [ps], [o_raw, tmaj(z), tmaj(glu)], (w_dw[0],) + norms + (hc,) + tail, n_tok * bs, 0,
        jax.ShapeDtypeStruct(hc.shape, F32), pl.BlockSpec(hc.shape, lambda i: (0, 0, 0)), [])
    y_sample = jnp.swapaxes(ys.reshape(n_tok, bs, d), 0, 1)

    return (y_prompt.reshape(bp, seq, d), y_sample, ssm_p[None], qkv_p[None], conf_p[None], ssm_s[None],
            jnp.swapaxes(qkv_s, 0, 1)[None], jnp.swapaxes(conf_s, 0, 1)[None])
```

```python
import functools

import jax
import jax.numpy as jnp
from jax import lax
from jax.experimental import pallas as pl
from jax.experimental.pallas import tpu as pltpu

F32 = jnp.float32
BF16 = jnp.bfloat16
EPS = 1e-6
HEAD_DIM = 64
CHUNK = 64
GROUP_W = 256
HEADS_PER_GROUP = GROUP_W // HEAD_DIM
SUBLANES = 8
VMEM_LIMIT = 56 * 1024 * 1024


def _dot(a, b):
    return jnp.dot(a.astype(BF16), b.astype(BF16), preferred_element_type=F32)


def _dot_nt(a, b):
    return lax.dot_general(a.astype(BF16), b.astype(BF16), (((1,), (1,)), ((), ())), preferred_element_type=F32)


def _sigmoid(x):
    return 1.0 / (1.0 + jnp.exp(-x))


def _silu(x):
    return x * _sigmoid(x)


def _softplus(x):
    return jnp.maximum(x, 0.0) + jnp.log1p(jnp.exp(-jnp.abs(x)))


def _rms(x, gain):
    return x * lax.rsqrt(jnp.mean(x * x, axis=-1, keepdims=True) + EPS) * gain


def _split3(x):
    hi = x.astype(BF16)
    r = x - hi.astype(F32)
    mid = r.astype(BF16)
    lo = (r - mid.astype(F32)).astype(BF16)
    return hi, mid, lo


def _dot_exact(a01, x):
    hi, mid, lo = _split3(x)
    d = lambda p: jnp.dot(a01, p, preferred_element_type=F32)
    return d(hi) + (d(mid) + d(lo))


def _block_ones(n):
    shift = HEAD_DIM.bit_length() - 1
    r = lax.broadcasted_iota(jnp.int32, (n, n), 0) >> shift
    c = lax.broadcasted_iota(jnp.int32, (n, n), 1) >> shift
    return (r == c).astype(BF16)


def _group_sum(x, ones_bd):
    outs = [jnp.dot(x[:, GROUP_W * b:GROUP_W * (b + 1)].astype(BF16), ones_bd, preferred_element_type=F32)
            for b in range(x.shape[1] // GROUP_W)]
    return jnp.concatenate(outs, axis=1)


def _expand_heads(cols, off, n_heads):
    rows = cols.shape[0]
    lo_half = lax.broadcasted_iota(jnp.int32, (rows, 128), 1) < HEAD_DIM
    blocks = []
    for b in range(n_heads // 2):
        c0 = cols[:, off + 2 * b:off + 2 * b + 1]
        c1 = cols[:, off + 2 * b + 1:off + 2 * b + 2]
        blocks.append(jnp.where(lo_half, c0, c1))
    return jnp.concatenate(blocks, axis=1)


class _Masks:
    SHAPES = dict(strict=((CHUNK, GROUP_W), F32), causal=((CHUNK, GROUP_W), F32), eye_cat=((CHUNK, GROUP_W), F32),
                  bd=((GROUP_W, GROUP_W), BF16), bd_f32=((GROUP_W, GROUP_W), F32), tri=((CHUNK, CHUNK), BF16),
                  ones_cc=((CHUNK, CHUNK), BF16))

    @classmethod
    def scratch(cls):
        return [pltpu.VMEM(shape, dtype) for shape, dtype in cls.SHAPES.values()]

    def __init__(self, refs, pool_ref=None):
        self._refs = dict(zip(self.SHAPES, refs))
        self.pool = _BlockDiagPool(pool_ref)

    def __getattr__(self, name):
        return self._refs[name][...]

    def fill(self, seqs):
        c = CHUNK
        r = self._refs
        ri = lax.broadcasted_iota(jnp.int32, (c, GROUP_W), 0)
        lj = lax.broadcasted_iota(jnp.int32, (c, GROUP_W), 1) & (HEAD_DIM - 1)
        same = (ri & (seqs - 1)) == (lj & (seqs - 1))
        r["strict"][...] = (same & (ri > lj)).astype(F32)
        r["causal"][...] = (same & (ri >= lj)).astype(F32)
        r["eye_cat"][...] = (ri == lj).astype(F32)
        r["bd"][...] = _block_ones(GROUP_W)
        r["bd_f32"][...] = _block_ones(GROUP_W).astype(F32)
        r3 = lax.broadcasted_iota(jnp.int32, (c, c), 0)
        c3 = lax.broadcasted_iota(jnp.int32, (c, c), 1)
        r["tri"][...] = (((r3 & (seqs - 1)) == (c3 & (seqs - 1))) & (r3 >= c3)).astype(BF16)
        r["ones_cc"][...] = jnp.ones((c, c), BF16)


class _BlockDiagPool:
    def __init__(self, ref):
        self.ref = ref
        self.used = 0

    def zero(self):
        self.ref[...] = jnp.zeros(self.ref.shape, self.ref.dtype)

    def expand(self, y):
        n = self.used
        self.used += 1
        yb = y.astype(BF16)
        for h in range(HEADS_PER_GROUP):
            blk = slice(HEAD_DIM * h, HEAD_DIM * (h + 1))
            self.ref[n, blk, blk] = yb[:, blk]
        return self.ref[n]


def _block_diag(y, m):
    return m.pool.expand(y)


def _interleave(gens, weights):
    live = dict(zip(gens, weights))
    while live:
        for g, w in tuple(live.items()):
            for _ in range(w):
                try:
                    next(g)
                except StopIteration:
                    live.pop(g, None)
                    break


def _zero_from(v, width):
    bits = lax.bitcast_convert_type(v[0:1], jnp.int32)
    half = jnp.full(bits.shape, 16, jnp.int32)
    z = lax.shift_right_logical(lax.shift_right_logical(bits, half), half).astype(F32)
    return jnp.concatenate([z] * (width // z.shape[1]), axis=1)


def _drain(gen):
    for _ in gen:
        pass


def _prep_stage(out, acc, ba, g_last_of, dn, n_heads, m):
    qkv = _silu(acc)
    yield
    q, k, v = qkv[:, :dn], qkv[:, dn:2 * dn], qkv[:, 2 * dn:]
    qs = q * (lax.rsqrt(_group_sum(q * q, m.bd) + EPS) * (HEAD_DIM ** -0.5))
    k = k * lax.rsqrt(_group_sum(k * k, m.bd) + EPS)
    yield
    gcs = _dot_exact(m.tri, ba)
    bexp = _expand_heads(ba, 0, n_heads)
    g_row = _expand_heads(gcs, n_heads, n_heads)
    g_last = g_last_of(g_row)
    yield
    egc = jnp.exp(g_row)
    kb = k * bexp
    vb = v * bexp
    kbd = kb * egc
    qd = qs * egc
    kd = k * jnp.exp(g_last - g_row)
    yield
    for grp in range(dn // GROUP_W):
        sl = slice(GROUP_W * grp, GROUP_W * (grp + 1))
        gr = g_row[:, sl]
        g_col = _dot_exact(m.ones_cc, gr * m.eye_cat)
        dec = jnp.exp(jnp.minimum(gr - g_col, 0.0)) * m.causal
        aq = _dot_nt(jnp.concatenate([kb[:, sl], qs[:, sl]], axis=0), _block_diag(k[:, sl], m))
        out.append(dict(a=aq[:CHUNK] * (dec * m.strict), qk=aq[CHUNK:] * dec,
                        rhs=jnp.concatenate([_block_diag(vb[:, sl], m), _block_diag(kbd[:, sl], m)], axis=1),
                        qd=qd[:, sl], kd=kd[:, sl], g_last=g_last[:, sl]))
        yield


def _solve_stage(probs, m, levels):
    pws = [-p["a"] for p in probs]
    ts = [m.eye_cat + pw for pw in pws]
    for _ in range(levels):
        pws = [_dot(pw, _block_diag(pw, m)) for pw in pws]
        ts = [t + _dot(t, _block_diag(pw, m)) for t, pw in zip(ts, pws)]
        yield
    for p, t in zip(probs, ts):
        uw = _dot(t, p["rhs"])
        p["u"], p["w"] = uw[:, :GROUP_W], uw[:, GROUP_W:]
    yield


def _inproj_kernel(n_heads, dn, cc, x_ref, gmix_ref, wt_ref, gp_ref, qkv_ref, z_ref, ba_ref, glu_ref):
    hb = _rms(x_ref[...], gmix_ref[...]).astype(BF16)
    d = lambda r0, r1: lax.dot_general(hb, wt_ref[r0:r1], (((1,), (1,)), ((), ())), preferred_element_type=F32)
    o_z = 3 * dn
    o_ba = o_z + dn
    o_ga = o_ba + 2 * n_heads
    o_gb = o_ga + cc
    qkv_ref[...] = d(0, o_z)
    z_ref[...] = _silu(d(o_z, o_ba))
    glu_ref[...] = d(o_ga, o_gb) * _sigmoid(d(o_gb, o_gb + cc))
    ba = d(o_ba, o_ga)
    col = lax.broadcasted_iota(jnp.int32, ba.shape, 1)
    g = -jnp.exp(gp_ref[0:1, :]) * _softplus(ba + gp_ref[1:2, :])
    ba_ref[...] = jnp.where(col < n_heads, _sigmoid(ba), g)


def _inproj(x, gmix, wt, gp, n_heads, dn, cc, tm):
    t, d = x.shape
    const = lambda shape: pl.BlockSpec(shape, lambda i: (0, 0), pipeline_mode=pl.Buffered(1))
    row = lambda w: pl.BlockSpec((tm, w), lambda i: (i, 0))
    return pl.pallas_call(
        functools.partial(_inproj_kernel, n_heads, dn, cc),
        grid=(t // tm,),
        in_specs=[row(d), const(gmix.shape), const(wt.shape), const(gp.shape)],
        out_specs=[row(3 * dn), row(dn), row(2 * n_heads), row(cc)],
        out_shape=[jax.ShapeDtypeStruct((t, 3 * dn), F32), jax.ShapeDtypeStruct((t, dn), F32),
                   jax.ShapeDtypeStruct((t, 2 * n_heads), F32), jax.ShapeDtypeStruct((t, cc), F32)],
        compiler_params=pltpu.CompilerParams(dimension_semantics=("arbitrary",), vmem_limit_bytes=VMEM_LIMIT),
        name="inproj",
    )(x, gmix, wt, gp)


def _seq_prompt_kernel(dn, n_short, qkv_ref, ba_ref, wshort_ref, o_ref, sout_ref, qh_ref, hist, sbd, pool_ref,
                       *mask_refs):
    step = pl.program_id(0)
    n_seq = qkv_ref.shape[0]
    xpad = hist.shape[1]
    n_heads = dn // HEAD_DIM
    n_groups = dn // GROUP_W
    m = _Masks(mask_refs, pool_ref)
    levels = (CHUNK - 1).bit_length() - 1

    @pl.when(step == 0)
    def _():
        m.fill(1)
        m.pool.zero()
        hist[...] = jnp.zeros(hist.shape, F32)
        sbd[...] = jnp.zeros(sbd.shape, F32)

    probs = [[] for _ in range(n_seq)]

    def prep(b):
        window = jnp.concatenate([hist[b], qkv_ref[b]], axis=0)
        even = odd = None
        shifted = window
        for dist in range(0, n_short, 2):
            if dist:
                shifted = pltpu.roll(shifted, 2, 0)
            term = shifted * wshort_ref[n_short - 1 - dist:n_short - dist]
            even = term if even is None else even + term
            if dist + 1 < n_short:
                term = shifted * wshort_ref[n_short - 2 - dist:n_short - 1 - dist]
                odd = term if odd is None else odd + term
        acc = even if odd is None else even + pltpu.roll(odd, 1, 0)
        yield
        yield from _prep_stage(probs[b], acc[xpad:], ba_ref[b], lambda g_row: g_row[CHUNK - 1:CHUNK], dn, n_heads, m)

    def recur(b):
        o_parts = []
        for grp, p in enumerate(probs[b]):
            s = sbd[n_groups * b + grp]
            wq = _dot(jnp.concatenate([p["w"], p["qd"]], axis=0), s)
            v_new = p["u"] - wq[:CHUNK]
            yield
            o_parts.append(wq[CHUNK:] + _dot(p["qk"], _block_diag(v_new, m)))
            sbd[n_groups * b + grp] = s * jnp.exp(p["g_last"]) + _dot(p["kd"].T, v_new) * m.bd_f32
            yield
        o_ref[b] = jnp.concatenate(o_parts, axis=1)

    for b in range(n_seq):
        _drain(prep(b))
    _drain(_solve_stage([p for pb in probs for p in pb], m, levels))
    _interleave([recur(b) for b in range(n_seq)], [1] * n_seq)

    for b in range(n_seq):
        tail = qkv_ref[b, CHUNK - xpad:CHUNK]
        hist[b] = tail
        qh_ref[b] = tail[xpad - (n_short - 1):]

    @pl.when(step == pl.num_programs(0) - 1)
    def _():
        for b in range(n_seq):
            for h in range(n_heads):
                grp, hl = divmod(h, HEADS_PER_GROUP)
                sout_ref[b, h] = sbd[n_groups * b + grp, HEAD_DIM * hl:HEAD_DIM * (hl + 1),
                                     HEAD_DIM * hl:HEAD_DIM * (hl + 1)]


def _bd_uses(n_chunks, n_groups, levels=(CHUNK - 1).bit_length() - 1):
    return n_chunks * n_groups * (1 + 2 + 2 * levels + 1)


def _seq_prompt(qkv, ba, wshort, batch):
    t, dn3 = qkv.shape
    dn = dn3 // 3
    seq = t // batch
    n_heads = dn // HEAD_DIM
    n_groups = dn // GROUP_W
    n_short = wshort.shape[0]
    xpad = -(-(n_short - 1) // SUBLANES) * SUBLANES
    chunk_of = lambda w: pl.BlockSpec((batch, CHUNK, w), lambda c: (0, c, 0))
    whole = lambda shape: pl.BlockSpec(shape, lambda c: (0,) * len(shape))
    o, sout, qh = pl.pallas_call(
        functools.partial(_seq_prompt_kernel, dn, n_short),
        grid=(seq // CHUNK,),
        in_specs=[chunk_of(dn3), chunk_of(ba.shape[1]), whole(wshort.shape)],
        out_specs=[chunk_of(dn), whole((batch, n_heads, HEAD_DIM, HEAD_DIM)), whole((batch, n_short - 1, dn3))],
        out_shape=[jax.ShapeDtypeStruct((batch, seq, dn), F32),
                   jax.ShapeDtypeStruct((batch, n_heads, HEAD_DIM, HEAD_DIM), F32),
                   jax.ShapeDtypeStruct((batch, n_short - 1, dn3), F32)],
        scratch_shapes=[pltpu.VMEM((batch, xpad, dn3), F32), pltpu.VMEM((batch * n_groups, GROUP_W, GROUP_W), F32),
                        pltpu.VMEM((_bd_uses(batch, n_groups), GROUP_W, GROUP_W), BF16)] + _Masks.scratch(),
        compiler_params=pltpu.CompilerParams(dimension_semantics=("arbitrary",), vmem_limit_bytes=VMEM_LIMIT),
        name="seq_prompt",
    )(qkv.reshape(batch, seq, dn3), ba.reshape(batch, seq, ba.shape[1]), wshort)
    return o.reshape(t, dn), sout, qh


def _seq_sample_kernel(dn, hq_ref, xq_ref, ba_ref, s0_ref, wshort_ref, o_ref, sout_ref, nq_ref, pool_ref, *mask_refs):
    n_tok, seqs = xq_ref.shape[0], xq_ref.shape[1]
    n_heads = dn // HEAD_DIM
    n_short = wshort_ref.shape[0]
    m = _Masks(mask_refs, pool_ref)
    levels = max((n_tok - 1).bit_length() - 1, 0)

    @pl.when(pl.program_id(0) == 0)
    def _():
        m.fill(seqs)
        m.pool.zero()

    full_q = [hq_ref[j] for j in range(n_short - 1)] + [xq_ref[t] for t in range(n_tok)]
    for j in range(n_short - 1):
        nq_ref[j] = full_q[n_tok + j]
    rows = []
    for t in range(n_tok):
        acc = full_q[t] * wshort_ref[0:1]
        for j in range(1, n_short):
            acc = acc + full_q[t + j] * wshort_ref[j:j + 1]
        rows.append(acc)
    ba = jnp.concatenate([ba_ref[t] for t in range(n_tok)], axis=0)
    probs = []
    g_last_of = lambda g_row: jnp.concatenate([g_row[(n_tok - 1) * seqs:]] * n_tok, axis=0)
    _drain(_prep_stage(probs, jnp.concatenate(rows, axis=0), ba, g_last_of, dn, n_heads, m))
    _drain(_solve_stage(probs, m, levels))

    row_seq = lax.broadcasted_iota(jnp.int32, (2 * CHUNK, GROUP_W), 0) & (seqs - 1)
    col_seq = lax.broadcasted_iota(jnp.int32, (GROUP_W, CHUNK), 1) & (seqs - 1)
    o_parts = []
    for grp, p in enumerate(probs):
        lhs = jnp.concatenate([p["w"], p["qd"]], axis=0).astype(BF16)
        s_bd = []
        wq = jnp.zeros((2 * CHUNK, GROUP_W), F32)
        for s in range(seqs):
            s_rows = s0_ref[s, HEADS_PER_GROUP * grp:HEADS_PER_GROUP * (grp + 1)].reshape(GROUP_W, HEAD_DIM)
            s_full = jnp.concatenate([s_rows] * HEADS_PER_GROUP, axis=1) * m.bd_f32
            s_bd.append(s_full)
            wq = jnp.where(row_seq == s, _dot(lhs, s_full), wq)
        v_new = p["u"] - wq[:CHUNK]
        o_parts.append(wq[CHUNK:] + _dot(p["qk"], _block_diag(v_new, m)))
        kd_t = p["kd"].T.astype(BF16)
        for s in range(seqs):
            upd = _dot(jnp.where(col_seq == s, kd_t, jnp.zeros((), BF16)), v_new)
            s_new = s_bd[s] * jnp.exp(p["g_last"][s:s + 1]) + upd * m.bd_f32
            for hl in range(HEADS_PER_GROUP):
                sout_ref[s, HEADS_PER_GROUP * grp + hl] = s_new[HEAD_DIM * hl:HEAD_DIM * (hl + 1),
                                                                 HEAD_DIM * hl:HEAD_DIM * (hl + 1)]
    o = jnp.concatenate(o_parts, axis=1)
    for t in range(n_tok):
        o_ref[t] = o[t * seqs:(t + 1) * seqs]


def _seq_sample(hq, xq, ba, s0, wshort):
    n_tok, nseq, dn3 = xq.shape
    dn = dn3 // 3
    seqs = CHUNK // n_tok
    n_heads = dn // HEAD_DIM
    tm = lambda a: pl.BlockSpec((a.shape[0], seqs, a.shape[2]), lambda i: (0, i, 0))
    st = pl.BlockSpec((seqs, n_heads, HEAD_DIM, HEAD_DIM), lambda i: (i, 0, 0, 0))
    o_shape = jax.ShapeDtypeStruct((n_tok, nseq, dn), F32)
    return pl.pallas_call(
        functools.partial(_seq_sample_kernel, dn),
        grid=(nseq // seqs,),
        in_specs=[tm(hq), tm(xq), tm(ba), st, pl.BlockSpec(wshort.shape, lambda i: (0, 0))],
        out_specs=[tm(o_shape), st, tm(hq)],
        out_shape=[o_shape, jax.ShapeDtypeStruct(s0.shape, F32), jax.ShapeDtypeStruct(hq.shape, F32)],
        scratch_shapes=[pltpu.VMEM((_bd_uses(1, dn // GROUP_W, max((n_tok - 1).bit_length() - 1, 0)), GROUP_W, GROUP_W), BF16)]
        + _Masks.scratch(),
        compiler_params=pltpu.CompilerParams(dimension_semantics=("arbitrary",), vmem_limit_bytes=VMEM_LIMIT),
        name="seq_sample",
    )(hq, xq, ba, s0, wshort)


def _o_norm(o_raw, z_silu, go_ref, ones_bd):
    ms = _group_sum(o_raw * o_raw, ones_bd) * (1.0 / HEAD_DIM)
    return o_raw * lax.rsqrt(ms + EPS) * go_ref[...] * z_silu


def _conv_norm(conv, bdw_ref, gcln_ref, bcln_ref, ones_bd):
    inv = 1.0 / HEAD_DIM
    cv = conv + bdw_ref[...]
    d = cv - _group_sum(cv, ones_bd) * inv
    return _silu(d * lax.rsqrt(_group_sum(d * d, ones_bd) * inv + EPS) * gcln_ref[...] + bcln_ref[...])


def _fold_rows(x, prev=None):
    acc = x[0:SUBLANES] if prev is None else x[0:SUBLANES] + prev
    for r in range(SUBLANES, x.shape[0], SUBLANES):
        acc = acc + x[r:r + SUBLANES]
    return acc


def _tail_stage(y_ref, x, ocv, p, wout_ref, gmlp_ref, wup_ref, wdown_ref, gple_ref, wgate_ref, wple_ref, gfin_ref,
                ff_chunk, anchor=lambda: 0.0):
    def dot_cols(a, w_ref, r0=0, r1=None, c0=0, c1=None):
        a = a.astype(BF16)
        r1 = w_ref.shape[0] if r1 is None else r1
        c1 = w_ref.shape[1] if c1 is None else c1
        cols = []
        for c in range(c0, c1, GROUP_W):
            cols.append(jnp.dot(a, w_ref[r0:r1, c:c + GROUP_W], preferred_element_type=F32))
            yield
        return jnp.concatenate(cols, axis=1)

    x = x + (yield from dot_cols(ocv, wout_ref))
    hb = _rms(x, gmlp_ref[...] + anchor()).astype(BF16)
    mlp = None
    for f in range(wup_ref.shape[1] // ff_chunk):
        up = jnp.maximum((yield from dot_cols(hb, wup_ref, c0=f * ff_chunk, c1=(f + 1) * ff_chunk)), anchor())
        down = yield from dot_cols(up * up, wdown_ref, r0=f * ff_chunk, r1=(f + 1) * ff_chunk)
        mlp = down if mlp is None else mlp + down
    x = x + mlp
    gate = _sigmoid((yield from dot_cols(_rms(x, gple_ref[...] + anchor()), wgate_ref)))
    x = x + (yield from dot_cols(p, wple_ref)) * gate
    y_ref[...] = _rms(x, gfin_ref[...] + anchor())


def _mix_stage(progress, ocv_buf, nc_ref, o_ref, z_ref, glu_ref, wdw_ref, norm_refs, ones_ref, cbuf, tm, block):
    n_dw = wdw_ref.shape[0]
    cpad = cbuf.shape[0] - tm - SUBLANES
    first = cpad - (n_dw - 1)
    go_ref, bdw_ref, gcln_ref, bcln_ref = norm_refs
    cbuf[cpad:cpad + tm] = glu_ref[...]
    o = _o_norm(o_ref[...], z_ref[...], go_ref, ones_ref[...])
    progress.append(_fold_rows(o))
    yield
    blocks = []
    for r0 in range(0, tm, block):
        conv = None
        for s in range(SUBLANES):
            part = None
            for a in range((first + n_dw - 1) // SUBLANES + 1):
                j = SUBLANES * a + s - first
                if 0 <= j < n_dw:
                    lo = SUBLANES * a + r0
                    term = cbuf[lo:lo + block + SUBLANES] * wdw_ref[j:j + 1]
                    part = term if part is None else part + term
            if part is not None:
                conv = part[s:s + block] if conv is None else conv + part[s:s + block]
            yield
        blocks.append(conv)
        progress.append(_fold_rows(conv, progress[-1]))
    cbuf[0:cpad] = cbuf[tm:tm + cpad]
    nc_ref[0] = cbuf[cpad - (n_dw - 1):cpad]
    cv = _conv_norm(jnp.concatenate(blocks, axis=0), bdw_ref, gcln_ref, bcln_ref, ones_ref[...])
    progress.append(_fold_rows(cv, progress[-1]))
    ocv_buf[...] = jnp.concatenate([o, cv], axis=1).astype(BF16)


def _out_prompt_kernel(tm, tiles_per_seq, ff_chunk, x_ref, p_ref, o_ref, z_ref, glu_ref, wdw_ref, go_ref, bdw_ref,
                       gcln_ref, bcln_ref, *rest):
    *tail_refs, y_ref, nc_ref, cbuf, ocv_buf, ones_ref = rest
    step = pl.program_id(0)
    cpad = cbuf.shape[0] - tm - SUBLANES

    @pl.when(step == 0)
    def _():
        ones_ref[...] = _block_ones(GROUP_W)
        cbuf[cpad + tm:] = jnp.zeros((SUBLANES, cbuf.shape[1]), F32)

    @pl.when(step % tiles_per_seq == 0)
    def _():
        cbuf[0:cpad] = jnp.zeros((cpad, cbuf.shape[1]), F32)

    mix = lambda progress: _mix_stage(progress, ocv_buf, nc_ref, o_ref, z_ref, glu_ref, wdw_ref,
                                      (go_ref, bdw_ref, gcln_ref, bcln_ref), ones_ref, cbuf, tm, CHUNK)

    @pl.when(step == 0)
    def _():
        _drain(mix([]))

    @pl.when(step > 0)
    def _():
        progress = []
        anchor = lambda: _zero_from(progress[-1], x_ref.shape[1]) if progress else 0.0
        _interleave([_tail_stage(y_ref, x_ref[...], ocv_buf[...], p_ref[...], *tail_refs, ff_chunk, anchor),
                     mix(progress)], [2, 4])


def _out_sample_kernel(ff_chunk, x_ref, p_ref, o_ref, z_ref, glu_ref, wdw_ref, go_ref, bdw_ref, gcln_ref, bcln_ref,
                       hc_ref, *rest):
    *tail_refs, y_ref, nc_ref = rest
    n_tok = glu_ref.shape[0]
    n_dw = wdw_ref.shape[0]
    full_c = [hc_ref[j] for j in range(n_dw - 1)] + [glu_ref[t] for t in range(n_tok)]
    for j in range(n_dw - 1):
        nc_ref[j] = full_c[n_tok + j]
    rows = []
    for t in range(n_tok):
        acc = full_c[t] * wdw_ref[0:1]
        for j in range(1, n_dw):
            acc = acc + full_c[t + j] * wdw_ref[j:j + 1]
        rows.append(acc)
    cat = lambda ref: jnp.concatenate([ref[t] for t in range(n_tok)], axis=0)
    ones_bd = _block_ones(GROUP_W)
    ocv = jnp.concatenate([_o_norm(cat(o_ref), cat(z_ref), go_ref, ones_bd),
                           _conv_norm(jnp.concatenate(rows, axis=0), bdw_ref, gcln_ref, bcln_ref, ones_bd)],
                          axis=1).astype(BF16)
    _drain(_tail_stage(y_ref, x_ref[...], ocv, p_ref[...], *tail_refs, ff_chunk))


def _out_call(kernel_fn, name, x, behind, ahead, consts, tm, lag, extra_out_shape, extra_out_spec, scratch):
    t, d = x.shape
    nt = t // tm
    late = lambda i: jnp.maximum(i - lag, 0)
    early = lambda i: jnp.minimum(i, nt - 1)
    whole = lambda a, **kw: pl.BlockSpec(a.shape, lambda i: (0,) * a.ndim, **kw)
    tiled = lambda a, tile_of: (pl.BlockSpec((tm, a.shape[1]), lambda i: (tile_of(i), 0)) if a.ndim == 2
                                else whole(a))
    return pl.pallas_call(
        kernel_fn,
        grid=(nt + lag,),
        in_specs=[tiled(a, late) for a in (x, *behind)] + [tiled(a, early) for a in ahead]
        + [whole(a, pipeline_mode=pl.Buffered(1)) for a in consts],
        out_specs=[tiled(x, late), extra_out_spec],
        out_shape=[jax.ShapeDtypeStruct((t, d), F32), extra_out_shape],
        scratch_shapes=scratch,
        compiler_params=pltpu.CompilerParams(dimension_semantics=("arbitrary",), vmem_limit_bytes=VMEM_LIMIT),
        name=name,
    )(x, *behind, *ahead, *consts)


def kernel(x_prompt, x_sample, state_ssm, state_qkv_conv, state_conf_conv, p_prompt, p_sample, g_mix, w_in, w_short,
           a_log, dt_bias, g_o, w_dw, b_dw, g_cln, b_cln, w_out, g_mlp, w_up, w_down, w_ple, g_ple, w_ple_gate,
           g_final):
    depth = w_in.shape[0]
    assert depth == 1, "single-layer trunk"
    bp, seq, d = x_prompt.shape
    bs, n_tok, _ = x_sample.shape
    n_heads = a_log.shape[1]
    dn = n_heads * HEAD_DIM
    n_dw, cc = w_dw.shape[1], w_dw.shape[2]
    assert dn % GROUP_W == 0 and cc % GROUP_W == 0 and CHUNK % n_tok == 0
    tm_in, tm_out, ff_chunk = 512, 512, 1024
    assert seq % tm_out == 0 and seq % CHUNK == 0 and (bp * seq) % tm_in == 0

    wt = jnp.swapaxes(w_in[0], 0, 1).astype(BF16)
    gp = jnp.zeros((2, 2 * n_heads), F32).at[0, n_heads:].set(a_log[0]).at[1, n_heads:].set(dt_bias[0])
    row = lambda a: a.reshape(1, -1)
    norms = (row(jnp.tile(g_o[0], n_heads)), row(b_dw[0]), row(g_cln[0]), row(b_cln[0]))
    tail = (w_out[0].astype(BF16), row(g_mlp[0]), w_up[0].astype(BF16), w_down[0].astype(BF16), row(g_ple[0]),
            w_ple_gate[0].astype(BF16), w_ple[0].astype(BF16), row(g_final))

    xp = x_prompt.reshape(bp * seq, d)
    qkv, z, ba, glu = _inproj(xp, row(g_mix[0]), wt, gp, n_heads, dn, cc, tm=tm_in)
    o_raw, ssm_p, qkv_p = _seq_prompt(qkv, ba, w_short[0], batch=bp)
    cpad = -(-(n_dw - 1) // SUBLANES) * SUBLANES
    nt_out, tiles_per_seq = bp * seq // tm_out, seq // tm_out
    y_prompt, conf_p = _out_call(
        functools.partial(_out_prompt_kernel, tm_out, tiles_per_seq, ff_chunk), "out_prompt", xp,
        [p_prompt[0].reshape(bp * seq, -1)], [o_raw, z, glu], (w_dw[0],) + norms + tail, tm_out, 1,
        jax.ShapeDtypeStruct((bp, n_dw - 1, cc), F32),
        pl.BlockSpec((1, n_dw - 1, cc), lambda i: (jnp.minimum(i, nt_out - 1) // tiles_per_seq, 0, 0)),
        [pltpu.VMEM((cpad + tm_out + SUBLANES, cc), F32), pltpu.VMEM((tm_out, dn + cc), BF16),
         pltpu.VMEM((GROUP_W, GROUP_W), BF16)])

    xs = jnp.swapaxes(x_sample, 0, 1).reshape(n_tok * bs, d)
    qkv, z, ba, glu = _inproj(xs, row(g_mix[0]), wt, gp, n_heads, dn, cc, tm=n_tok * bs)
    tmaj = lambda a: a.reshape(n_tok, bs, -1)
    o_raw, ssm_s, qkv_s = _seq_sample(jnp.swapaxes(state_qkv_conv[0], 0, 1), tmaj(qkv), tmaj(ba), state_ssm[0],
                                      w_short[0])
    ps = jnp.swapaxes(p_sample[0], 0, 1).reshape(n_tok * bs, -1)
    hc = jnp.swapaxes(state_conf_conv[0], 0, 1)
    ys, conf_s = _out_call(
        functools.partial(_out_sample_kernel, ff_chunk), "out_sample", xs,
        [ps], [o_raw, tmaj(z), tmaj(glu)], (w_dw[0],) + norms + (hc,) + tail, n_tok * bs, 0,
        jax.ShapeDtypeStruct(hc.shape, F32), pl.BlockSpec(hc.shape, lambda i: (0, 0, 0)), [])
    y_sample = jnp.swapaxes(ys.reshape(n_tok, bs, d), 0, 1)

    return (y_prompt.reshape(bp, seq, d), y_sample, ssm_p[None], qkv_p[None], conf_p[None], ssm_s[None],
            jnp.swapaxes(qkv_s, 0, 1)[None], jnp.swapaxes(conf_s, 0, 1)[None])
```

```python
import functools

import jax
import jax.numpy as jnp
from jax import lax
from jax.experimental import pallas as pl
from jax.experimental.pallas import tpu as pltpu

F32 = jnp.float32
BF16 = jnp.bfloat16
EPS = 1e-6
HEAD_DIM = 64
CHUNK = 64
GROUP_W = 256
HEADS_PER_GROUP = GROUP_W // HEAD_DIM
SUBLANES = 8
VMEM_LIMIT = 56 * 1024 * 1024


def _dot(a, b):
    return jnp.dot(a.astype(BF16), b.astype(BF16), preferred_element_type=F32)


def _dot_nt(a, b):
    return lax.dot_general(a.astype(BF16), b.astype(BF16), (((1,), (1,)), ((), ())), preferred_element_type=F32)


def _sigmoid(x):
    return 1.0 / (1.0 + jnp.exp(-x))


def _silu(x):
    return x * _sigmoid(x)


def _softplus(x):
    return jnp.maximum(x, 0.0) + jnp.log1p(jnp.exp(-jnp.abs(x)))


def _rms(x, gain):
    return x * lax.rsqrt(jnp.mean(x * x, axis=-1, keepdims=True) + EPS) * gain


def _split3(x):
    hi = x.astype(BF16)
    r = x - hi.astype(F32)
    mid = r.astype(BF16)
    lo = (r - mid.astype(F32)).astype(BF16)
    return hi, mid, lo


def _dot_exact(a01, x):
    hi, mid, lo = _split3(x)
    d = lambda p: jnp.dot(a01, p, preferred_element_type=F32)
    return d(hi) + (d(mid) + d(lo))


def _block_ones(n):
    shift = HEAD_DIM.bit_length() - 1
    r = lax.broadcasted_iota(jnp.int32, (n, n), 0) >> shift
    c = lax.broadcasted_iota(jnp.int32, (n, n), 1) >> shift
    return (r == c).astype(BF16)


def _group_sum(x, ones_bd):
    outs = [jnp.dot(x[:, GROUP_W * b:GROUP_W * (b + 1)].astype(BF16), ones_bd, preferred_element_type=F32)
            for b in range(x.shape[1] // GROUP_W)]
    return jnp.concatenate(outs, axis=1)


def _expand_heads(cols, off, n_heads):
    rows = cols.shape[0]
    lo_half = lax.broadcasted_iota(jnp.int32, (rows, 128), 1) < HEAD_DIM
    blocks = []
    for b in range(n_heads // 2):
        c0 = cols[:, off + 2 * b:off + 2 * b + 1]
        c1 = cols[:, off + 2 * b + 1:off + 2 * b + 2]
        blocks.append(jnp.where(lo_half, c0, c1))
    return jnp.concatenate(blocks, axis=1)


class _Masks:
    SHAPES = dict(strict=((CHUNK, GROUP_W), F32), causal=((CHUNK, GROUP_W), F32), eye_cat=((CHUNK, GROUP_W), F32),
                  bd=((GROUP_W, GROUP_W), BF16), bd_f32=((GROUP_W, GROUP_W), F32), tri=((CHUNK, CHUNK), BF16),
                  ones_cc=((CHUNK, CHUNK), BF16))

    @classmethod
    def scratch(cls):
        return [pltpu.VMEM(shape, dtype) for shape, dtype in cls.SHAPES.values()]

    def __init__(self, refs, pool_ref=None):
        self._refs = dict(zip(self.SHAPES, refs))
        self.pool = _BlockDiagPool(pool_ref)

    def __getattr__(self, name):
        return self._refs[name][...]

    def fill(self, seqs):
        c = CHUNK
        r = self._refs
        ri = lax.broadcasted_iota(jnp.int32, (c, GROUP_W), 0)
        lj = lax.broadcasted_iota(jnp.int32, (c, GROUP_W), 1) & (HEAD_DIM - 1)
        same = (ri & (seqs - 1)) == (lj & (seqs - 1))
        r["strict"][...] = (same & (ri > lj)).astype(F32)
        r["causal"][...] = (same & (ri >= lj)).astype(F32)
        r["eye_cat"][...] = (ri == lj).astype(F32)
        r["bd"][...] = _block_ones(GROUP_W)
        r["bd_f32"][...] = _block_ones(GROUP_W).astype(F32)
        r3 = lax.broadcasted_iota(jnp.int32, (c, c), 0)
        c3 = lax.broadcasted_iota(jnp.int32, (c, c), 1)
        r["tri"][...] = (((r3 & (seqs - 1)) == (c3 & (seqs - 1))) & (r3 >= c3)).astype(BF16)
        r["ones_cc"][...] = jnp.ones((c, c), BF16)


class _BlockDiagPool:
    def __init__(self, ref):
        self.ref = ref
        self.used = 0

    def zero(self):
        self.ref[...] = jnp.zeros(self.ref.shape, self.ref.dtype)

    def expand(self, y):
        n = self.used
        self.used += 1
        yb = y.astype(BF16)
        for h in range(HEADS_PER_GROUP):
            blk = slice(HEAD_DIM * h, HEAD_DIM * (h + 1))
            self.ref[n, blk, blk] = yb[:, blk]
        return self.ref[n]


def _block_diag(y, m):
    return m.pool.expand(y)


def _interleave(gens, weights):
    live = dict(zip(gens, weights))
    while live:
        for g, w in tuple(live.items()):
            for _ in range(w):
                try:
                    next(g)
                except StopIteration:
                    live.pop(g, None)
                    break


def _zero_from(v, width):
    bits = lax.bitcast_convert_type(v[0:1], jnp.int32)
    half = jnp.full(bits.shape, 16, jnp.int32)
    z = lax.shift_right_logical(lax.shift_right_logical(bits, half), half).astype(F32)
    return jnp.concatenate([z] * (width // z.shape[1]), axis=1)


def _drain(gen):
    for _ in gen:
        pass


def _prep_stage(out, acc, ba, g_last_of, dn, n_heads, m):
    qkv = _silu(acc)
    yield
    q, k, v = qkv[:, :dn], qkv[:, dn:2 * dn], qkv[:, 2 * dn:]
    qs = q * (lax.rsqrt(_group_sum(q * q, m.bd) + EPS) * (HEAD_DIM ** -0.5))
    k = k * lax.rsqrt(_group_sum(k * k, m.bd) + EPS)
    yield
    gcs = _dot_exact(m.tri, ba)
    bexp = _expand_heads(ba, 0, n_heads)
    g_row = _expand_heads(gcs, n_heads, n_heads)
    g_last = g_last_of(g_row)
    yield
    egc = jnp.exp(g_row)
    kb = k * bexp
    vb = v * bexp
    kbd = kb * egc
    qd = qs * egc
    kd = k * jnp.exp(g_last - g_row)
    yield
    for grp in range(dn // GROUP_W):
        sl = slice(GROUP_W * grp, GROUP_W * (grp + 1))
        gr = g_row[:, sl]
        g_col = _dot_exact(m.ones_cc, gr * m.eye_cat)
        dec = jnp.exp(jnp.minimum(gr - g_col, 0.0)) * m.causal
        aq = _dot_nt(jnp.concatenate([kb[:, sl], qs[:, sl]], axis=0), _block_diag(k[:, sl], m))
        out.append(dict(a=aq[:CHUNK] * (dec * m.strict), qk=aq[CHUNK:] * dec,
                        rhs=jnp.concatenate([_block_diag(vb[:, sl], m), _block_diag(kbd[:, sl], m)], axis=1),
                        qd=qd[:, sl], kd=kd[:, sl], g_last=g_last[:, sl]))
        yield


def _solve_stage(probs, m, levels):
    pws = [-p["a"] for p in probs]
    ts = [m.eye_cat + pw for pw in pws]
    for lvl in range(levels):
        last = lvl == levels - 1
        ops = [_block_diag(pw, m) for pw in pws]
        r1 = [_dot(t if last else jnp.concatenate([pw, t], axis=0), op) for pw, t, op in zip(pws, ts, ops)]
        if not last:
            pws = [r[:CHUNK] for r in r1]
        ts = [t + _dot(r[-CHUNK:], op) for t, r, op in zip(ts, r1, ops)]
        yield
    for p, t in zip(probs, ts):
        uw = _dot(t, p["rhs"])
        p["u"], p["w"] = uw[:, :GROUP_W], uw[:, GROUP_W:]
    yield


def _inproj_kernel(n_heads, dn, cc, x_ref, gmix_ref, wt_ref, gp_ref, qkv_ref, z_ref, ba_ref, glu_ref):
    hb = _rms(x_ref[...], gmix_ref[...]).astype(BF16)
    d = lambda r0, r1: lax.dot_general(hb, wt_ref[r0:r1], (((1,), (1,)), ((), ())), preferred_element_type=F32)
    o_z = 3 * dn
    o_ba = o_z + dn
    o_ga = o_ba + 2 * n_heads
    o_gb = o_ga + cc
    qkv_ref[...] = d(0, o_z)
    z_ref[...] = _silu(d(o_z, o_ba))
    glu_ref[...] = d(o_ga, o_gb) * _sigmoid(d(o_gb, o_gb + cc))
    ba = d(o_ba, o_ga)
    col = lax.broadcasted_iota(jnp.int32, ba.shape, 1)
    g = -jnp.exp(gp_ref[0:1, :]) * _softplus(ba + gp_ref[1:2, :])
    ba_ref[...] = jnp.where(col < n_heads, _sigmoid(ba), g)


def _inproj(x, gmix, wt, gp, n_heads, dn, cc, tm):
    t, d = x.shape
    const = lambda shape: pl.BlockSpec(shape, lambda i: (0, 0), pipeline_mode=pl.Buffered(1))
    row = lambda w: pl.BlockSpec((tm, w), lambda i: (i, 0))
    return pl.pallas_call(
        functools.partial(_inproj_kernel, n_heads, dn, cc),
        grid=(t // tm,),
        in_specs=[row(d), const(gmix.shape), const(wt.shape), const(gp.shape)],
        out_specs=[row(3 * dn), row(dn), row(2 * n_heads), row(cc)],
        out_shape=[jax.ShapeDtypeStruct((t, 3 * dn), F32), jax.ShapeDtypeStruct((t, dn), F32),
                   jax.ShapeDtypeStruct((t, 2 * n_heads), F32), jax.ShapeDtypeStruct((t, cc), F32)],
        compiler_params=pltpu.CompilerParams(dimension_semantics=("arbitrary",), vmem_limit_bytes=VMEM_LIMIT),
        name="inproj",
    )(x, gmix, wt, gp)


def _seq_prompt_kernel(dn, n_short, qkv_ref, ba_ref, wshort_ref, o_ref, sout_ref, qh_ref, hist, sbd, pool_ref,
                       *mask_refs):
    step = pl.program_id(0)
    n_seq = qkv_ref.shape[0]
    xpad = hist.shape[1]
    n_heads = dn // HEAD_DIM
    n_groups = dn // GROUP_W
    m = _Masks(mask_refs, pool_ref)
    levels = (CHUNK - 1).bit_length() - 1

    @pl.when(step == 0)
    def _():
        m.fill(1)
        m.pool.zero()
        hist[...] = jnp.zeros(hist.shape, F32)
        sbd[...] = jnp.zeros(sbd.shape, F32)

    probs = [[] for _ in range(n_seq)]

    def prep(b):
        window = jnp.concatenate([hist[b], qkv_ref[b]], axis=0)
        even = odd = None
        shifted = window
        for dist in range(0, n_short, 2):
            if dist:
                shifted = pltpu.roll(shifted, 2, 0)
            term = shifted * wshort_ref[n_short - 1 - dist:n_short - dist]
            even = term if even is None else even + term
            if dist + 1 < n_short:
                term = shifted * wshort_ref[n_short - 2 - dist:n_short - 1 - dist]
                odd = term if odd is None else odd + term
        acc = even if odd is None else even + pltpu.roll(odd, 1, 0)
        yield
        yield from _prep_stage(probs[b], acc[xpad:], ba_ref[b], lambda g_row: g_row[CHUNK - 1:CHUNK], dn, n_heads, m)

    def recur(b):
        o_parts = []
        for grp, p in enumerate(probs[b]):
            s = sbd[n_groups * b + grp]
            wq = _dot(jnp.concatenate([p["w"], p["qd"]], axis=0), s)
            v_new = p["u"] - wq[:CHUNK]
            yield
            o_parts.append(wq[CHUNK:] + _dot(p["qk"], _block_diag(v_new, m)))
            sbd[n_groups * b + grp] = s * jnp.exp(p["g_last"]) + _dot(p["kd"].T, v_new) * m.bd_f32
            yield
        o_ref[b] = jnp.concatenate(o_parts, axis=1)

    for b in range(n_seq):
        _drain(prep(b))
    _drain(_solve_stage([p for pb in probs for p in pb], m, levels))
    _interleave([recur(b) for b in range(n_seq)], [1] * n_seq)

    for b in range(n_seq):
        tail = qkv_ref[b, CHUNK - xpad:CHUNK]
        hist[b] = tail
        qh_ref[b] = tail[xpad - (n_short - 1):]

    @pl.when(step == pl.num_programs(0) - 1)
    def _():
        for b in range(n_seq):
            for h in range(n_heads):
                grp, hl = divmod(h, HEADS_PER_GROUP)
                sout_ref[b, h] = sbd[n_groups * b + grp, HEAD_DIM * hl:HEAD_DIM * (hl + 1),
                                     HEAD_DIM * hl:HEAD_DIM * (hl + 1)]


def _bd_uses(n_chunks, n_groups, levels=(CHUNK - 1).bit_length() - 1):
    return n_chunks * n_groups * (1 + 2 + levels + 1)


def _seq_prompt(qkv, ba, wshort, batch):
    t, dn3 = qkv.shape
    dn = dn3 // 3
    seq = t // batch
    n_heads = dn // HEAD_DIM
    n_groups = dn // GROUP_W
    n_short = wshort.shape[0]
    xpad = -(-(n_short - 1) // SUBLANES) * SUBLANES
    chunk_of = lambda w: pl.BlockSpec((batch, CHUNK, w), lambda c: (0, c, 0))
    whole = lambda shape: pl.BlockSpec(shape, lambda c: (0,) * len(shape))
    o, sout, qh = pl.pallas_call(
        functools.partial(_seq_prompt_kernel, dn, n_short),
        grid=(seq // CHUNK,),
        in_specs=[chunk_of(dn3), chunk_of(ba.shape[1]), whole(wshort.shape)],
        out_specs=[chunk_of(dn), whole((batch, n_heads, HEAD_DIM, HEAD_DIM)), whole((batch, n_short - 1, dn3))],
        out_shape=[jax.ShapeDtypeStruct((batch, seq, dn), F32),
                   jax.ShapeDtypeStruct((batch, n_heads, HEAD_DIM, HEAD_DIM), F32),
                   jax.ShapeDtypeStruct((batch, n_short - 1, dn3), F32)],
        scratch_shapes=[pltpu.VMEM((batch, xpad, dn3), F32), pltpu.VMEM((batch * n_groups, GROUP_W, GROUP_W), F32),
                        pltpu.VMEM((_bd_uses(batch, n_groups), GROUP_W, GROUP_W), BF16)] + _Masks.scratch(),
        compiler_params=pltpu.CompilerParams(dimension_semantics=("arbitrary",), vmem_limit_bytes=VMEM_LIMIT),
        name="seq_prompt",
    )(qkv.reshape(batch, seq, dn3), ba.reshape(batch, seq, ba.shape[1]), wshort)
    return o.reshape(t, dn), sout, qh


def _seq_sample_kernel(dn, hq_ref, xq_ref, ba_ref, s0_ref, wshort_ref, o_ref, sout_ref, nq_ref, pool_ref, *mask_refs):
    n_tok, seqs = xq_ref.shape[0], xq_ref.shape[1]
    n_heads = dn // HEAD_DIM
    n_short = wshort_ref.shape[0]
    m = _Masks(mask_refs, pool_ref)
    levels = max((n_tok - 1).bit_length() - 1, 0)

    @pl.when(pl.program_id(0) == 0)
    def _():
        m.fill(seqs)
        m.pool.zero()

    full_q = [hq_ref[j] for j in range(n_short - 1)] + [xq_ref[t] for t in range(n_tok)]
    for j in range(n_short - 1):
        nq_ref[j] = full_q[n_tok + j]
    rows = []
    for t in range(n_tok):
        acc = full_q[t] * wshort_ref[0:1]
        for j in range(1, n_short):
            acc = acc + full_q[t + j] * wshort_ref[j:j + 1]
        rows.append(acc)
    ba = jnp.concatenate([ba_ref[t] for t in range(n_tok)], axis=0)
    probs = []
    g_last_of = lambda g_row: jnp.concatenate([g_row[(n_tok - 1) * seqs:]] * n_tok, axis=0)
    _drain(_prep_stage(probs, jnp.concatenate(rows, axis=0), ba, g_last_of, dn, n_heads, m))
    _drain(_solve_stage(probs, m, levels))

    row_seq = lax.broadcasted_iota(jnp.int32, (2 * CHUNK, GROUP_W), 0) & (seqs - 1)
    col_seq = lax.broadcasted_iota(jnp.int32, (GROUP_W, CHUNK), 1) & (seqs - 1)
    o_parts = []
    for grp, p in enumerate(probs):
        lhs = jnp.concatenate([p["w"], p["qd"]], axis=0).astype(BF16)
        s_bd = []
        wq = jnp.zeros((2 * CHUNK, GROUP_W), F32)
        for s in range(seqs):
            s_rows = s0_ref[s, HEADS_PER_GROUP * grp:HEADS_PER_GROUP * (grp + 1)].reshape(GROUP_W, HEAD_DIM)
            s_full = jnp.concatenate([s_rows] * HEADS_PER_GROUP, axis=1) * m.bd_f32
            s_bd.append(s_full)
            wq = jnp.where(row_seq == s, _dot(lhs, s_full), wq)
        v_new = p["u"] - wq[:CHUNK]
        o_parts.append(wq[CHUNK:] + _dot(p["qk"], _block_diag(v_new, m)))
        kd_t = p["kd"].T.astype(BF16)
        for s in range(seqs):
            upd = _dot(jnp.where(col_seq == s, kd_t, jnp.zeros((), BF16)), v_new)
            s_new = s_bd[s] * jnp.exp(p["g_last"][s:s + 1]) + upd * m.bd_f32
            for hl in range(HEADS_PER_GROUP):
                sout_ref[s, HEADS_PER_GROUP * grp + hl] = s_new[HEAD_DIM * hl:HEAD_DIM * (hl + 1),
                                                                 HEAD_DIM * hl:HEAD_DIM * (hl + 1)]
    o = jnp.concatenate(o_parts, axis=1)
    for t in range(n_tok):
        o_ref[t] = o[t * seqs:(t + 1) * seqs]


def _seq_sample(hq, xq, ba, s0, wshort):
    n_tok, nseq, dn3 = xq.shape
    dn = dn3 // 3
    seqs = CHUNK // n_tok
    n_heads = dn // HEAD_DIM
    tm = lambda a: pl.BlockSpec((a.shape[0], seqs, a.shape[2]), lambda i: (0, i, 0))
    st = pl.BlockSpec((seqs, n_heads, HEAD_DIM, HEAD_DIM), lambda i: (i, 0, 0, 0))
    o_shape = jax.ShapeDtypeStruct((n_tok, nseq, dn), F32)
    return pl.pallas_call(
        functools.partial(_seq_sample_kernel, dn),
        grid=(nseq // seqs,),
        in_specs=[tm(hq), tm(xq), tm(ba), st, pl.BlockSpec(wshort.shape, lambda i: (0, 0))],
        out_specs=[tm(o_shape), st, tm(hq)],
        out_shape=[o_shape, jax.ShapeDtypeStruct(s0.shape, F32), jax.ShapeDtypeStruct(hq.shape, F32)],
        scratch_shapes=[pltpu.VMEM((_bd_uses(1, dn // GROUP_W, max((n_tok - 1).bit_length() - 1, 0)), GROUP_W, GROUP_W), BF16)]
        + _Masks.scratch(),
        compiler_params=pltpu.CompilerParams(dimension_semantics=("arbitrary",), vmem_limit_bytes=VMEM_LIMIT),
        name="seq_sample",
    )(hq, xq, ba, s0, wshort)


def _o_norm(o_raw, z_silu, go_ref, ones_bd):
    ms = _group_sum(o_raw * o_raw, ones_bd) * (1.0 / HEAD_DIM)
    return o_raw * lax.rsqrt(ms + EPS) * go_ref[...] * z_silu


def _conv_norm(conv, bdw_ref, gcln_ref, bcln_ref, ones_bd):
    inv = 1.0 / HEAD_DIM
    cv = conv + bdw_ref[...]
    d = cv - _group_sum(cv, ones_bd) * inv
    return _silu(d * lax.rsqrt(_group_sum(d * d, ones_bd) * inv + EPS) * gcln_ref[...] + bcln_ref[...])


def _fold_rows(x, prev=None):
    acc = x[0:SUBLANES] if prev is None else x[0:SUBLANES] + prev
    for r in range(SUBLANES, x.shape[0], SUBLANES):
        acc = acc + x[r:r + SUBLANES]
    return acc


def _tail_stage(y_ref, x, ocv, p, wout_ref, gmlp_ref, wup_ref, wdown_ref, gple_ref, wgate_ref, wple_ref, gfin_ref,
                ff_chunk, anchor=lambda: 0.0):
    def dot_cols(a, w_ref, r0=0, r1=None, c0=0, c1=None):
        a = a.astype(BF16)
        r1 = w_ref.shape[0] if r1 is None else r1
        c1 = w_ref.shape[1] if c1 is None else c1
        cols = []
        for c in range(c0, c1, GROUP_W):
            cols.append(jnp.dot(a, w_ref[r0:r1, c:c + GROUP_W], preferred_element_type=F32))
            yield
        return jnp.concatenate(cols, axis=1)

    x = x + (yield from dot_cols(ocv, wout_ref))
    hb = _rms(x, gmlp_ref[...] + anchor()).astype(BF16)
    mlp = None
    for f in range(wup_ref.shape[1] // ff_chunk):
        up = jnp.maximum((yield from dot_cols(hb, wup_ref, c0=f * ff_chunk, c1=(f + 1) * ff_chunk)), anchor())
        down = yield from dot_cols(up * up, wdown_ref, r0=f * ff_chunk, r1=(f + 1) * ff_chunk)
        mlp = down if mlp is None else mlp + down
    x = x + mlp
    gate = _sigmoid((yield from dot_cols(_rms(x, gple_ref[...] + anchor()), wgate_ref)))
    x = x + (yield from dot_cols(p, wple_ref)) * gate
    y_ref[...] = _rms(x, gfin_ref[...] + anchor())


def _mix_stage(progress, ocv_buf, nc_ref, o_ref, z_ref, glu_ref, wdw_ref, norm_refs, ones_ref, cbuf, tm, block):
    n_dw = wdw_ref.shape[0]
    cpad = cbuf.shape[0] - tm - SUBLANES
    first = cpad - (n_dw - 1)
    go_ref, bdw_ref, gcln_ref, bcln_ref = norm_refs
    cbuf[cpad:cpad + tm] = glu_ref[...]
    o = _o_norm(o_ref[...], z_ref[...], go_ref, ones_ref[...])
    progress.append(_fold_rows(o))
    yield
    blocks = []
    for r0 in range(0, tm, block):
        conv = None
        for s in range(SUBLANES):
            part = None
            for a in range((first + n_dw - 1) // SUBLANES + 1):
                j = SUBLANES * a + s - first
                if 0 <= j < n_dw:
                    lo = SUBLANES * a + r0
                    term = cbuf[lo:lo + block + SUBLANES] * wdw_ref[j:j + 1]
                    part = term if part is None else part + term
            if part is not None:
                conv = part[s:s + block] if conv is None else conv + part[s:s + block]
            yield
        blocks.append(conv)
        progress.append(_fold_rows(conv, progress[-1]))
    cbuf[0:cpad] = cbuf[tm:tm + cpad]
    nc_ref[0] = cbuf[cpad - (n_dw - 1):cpad]
    cv = _conv_norm(jnp.concatenate(blocks, axis=0), bdw_ref, gcln_ref, bcln_ref, ones_ref[...])
    progress.append(_fold_rows(cv, progress[-1]))
    ocv_buf[...] = jnp.concatenate([o, cv], axis=1).astype(BF16)


def _out_prompt_kernel(tm, tiles_per_seq, ff_chunk, x_ref, p_ref, o_ref, z_ref, glu_ref, wdw_ref, go_ref, bdw_ref,
                       gcln_ref, bcln_ref, *rest):
    *tail_refs, y_ref, nc_ref, cbuf, ocv_buf, ones_ref = rest
    step = pl.program_id(0)
    cpad = cbuf.shape[0] - tm - SUBLANES

    @pl.when(step == 0)
    def _():
        ones_ref[...] = _block_ones(GROUP_W)
        cbuf[cpad + tm:] = jnp.zeros((SUBLANES, cbuf.shape[1]), F32)

    @pl.when(step % tiles_per_seq == 0)
    def _():
        cbuf[0:cpad] = jnp.zeros((cpad, cbuf.shape[1]), F32)

    mix = lambda progress: _mix_stage(progress, ocv_buf, nc_ref, o_ref, z_ref, glu_ref, wdw_ref,
                                      (go_ref, bdw_ref, gcln_ref, bcln_ref), ones_ref, cbuf, tm, CHUNK)

    @pl.when(step == 0)
    def _():
        _drain(mix([]))

    @pl.when(step > 0)
    def _():
        progress = []
        anchor = lambda: _zero_from(progress[-1], x_ref.shape[1]) if progress else 0.0
        _interleave([_tail_stage(y_ref, x_ref[...], ocv_buf[...], p_ref[...], *tail_refs, ff_chunk, anchor),
                     mix(progress)], [2, 4])


def _out_sample_kernel(ff_chunk, x_ref, p_ref, o_ref, z_ref, glu_ref, wdw_ref, go_ref, bdw_ref, gcln_ref, bcln_ref,
                       hc_ref, *rest):
    *tail_refs, y_ref, nc_ref = rest
    n_tok = glu_ref.shape[0]
    n_dw = wdw_ref.shape[0]
    full_c = [hc_ref[j] for j in range(n_dw - 1)] + [glu_ref[t] for t in range(n_tok)]
    for j in range(n_dw - 1):
        nc_ref[j] = full_c[n_tok + j]
    rows = []
    for t in range(n_tok):
        acc = full_c[t] * wdw_ref[0:1]
        for j in range(1, n_dw):
            acc = acc + full_c[t + j] * wdw_ref[j:j + 1]
        rows.append(acc)
    cat = lambda ref: jnp.concatenate([ref[t] for t in range(n_tok)], axis=0)
    ones_bd = _block_ones(GROUP_W)
    ocv = jnp.concatenate([_o_norm(cat(o_ref), cat(z_ref), go_ref, ones_bd),
                           _conv_norm(jnp.concatenate(rows, axis=0), bdw_ref, gcln_ref, bcln_ref, ones_bd)],
                          axis=1).astype(BF16)
    _drain(_tail_stage(y_ref, x_ref[...], ocv, p_ref[...], *tail_refs, ff_chunk))


def _out_call(kernel_fn, name, x, behind, ahead, consts, tm, lag, extra_out_shape, extra_out_spec, scratch):
    t, d = x.shape
    nt = t // tm
    late = lambda i: jnp.maximum(i - lag, 0)
    early = lambda i: jnp.minimum(i, nt - 1)
    whole = lambda a, **kw: pl.BlockSpec(a.shape, lambda i: (0,) * a.ndim, **kw)
    tiled = lambda a, tile_of: (pl.BlockSpec((tm, a.shape[1]), lambda i: (tile_of(i), 0)) if a.ndim == 2
                                else whole(a))
    return pl.pallas_call(
        kernel_fn,
        grid=(nt + lag,),
        in_specs=[tiled(a, late) for a in (x, *behind)] + [tiled(a, early) for a in ahead]
        + [whole(a, pipeline_mode=pl.Buffered(1)) for a in consts],
        out_specs=[tiled(x, late), extra_out_spec],
        out_shape=[jax.ShapeDtypeStruct((t, d), F32), extra_out_shape],
        scratch_shapes=scratch,
        compiler_params=pltpu.CompilerParams(dimension_semantics=("arbitrary",), vmem_limit_bytes=VMEM_LIMIT),
        name=name,
    )(x, *behind, *ahead, *consts)


def kernel(x_prompt, x_sample, state_ssm, state_qkv_conv, state_conf_conv, p_prompt, p_sample, g_mix, w_in, w_short,
           a_log, dt_bias, g_o, w_dw, b_dw, g_cln, b_cln, w_out, g_mlp, w_up, w_down, w_ple, g_ple, w_ple_gate,
           g_final):
    depth = w_in.shape[0]
    assert depth == 1, "single-layer trunk"
    bp, seq, d = x_prompt.shape
    bs, n_tok, _ = x_sample.shape
    n_heads = a_log.shape[1]
    dn = n_heads * HEAD_DIM
    n_dw, cc = w_dw.shape[1], w_dw.shape[2]
    assert dn % GROUP_W == 0 and cc % GROUP_W == 0 and CHUNK % n_tok == 0
    tm_in, tm_out, ff_chunk = 512, 512, 1024
    assert seq % tm_out == 0 and seq % CHUNK == 0 and (bp * seq) % tm_in == 0

    wt = jnp.swapaxes(w_in[0], 0, 1).astype(BF16)
    gp = jnp.zeros((2, 2 * n_heads), F32).at[0, n_heads:].set(a_log[0]).at[1, n_heads:].set(dt_bias[0])
    row = lambda a: a.reshape(1, -1)
    norms = (row(jnp.tile(g_o[0], n_heads)), row(b_dw[0]), row(g_cln[0]), row(b_cln[0]))
    tail = (w_out[0].astype(BF16), row(g_mlp[0]), w_up[0].astype(BF16), w_down[0].astype(BF16), row(g_ple[0]),
            w_ple_gate[0].astype(BF16), w_ple[0].astype(BF16), row(g_final))

    xp = x_prompt.reshape(bp * seq, d)
    qkv, z, ba, glu = _inproj(xp, row(g_mix[0]), wt, gp, n_heads, dn, cc, tm=tm_in)
    o_raw, ssm_p, qkv_p = _seq_prompt(qkv, ba, w_short[0], batch=bp)
    cpad = -(-(n_dw - 1) // SUBLANES) * SUBLANES
    nt_out, tiles_per_seq = bp * seq // tm_out, seq // tm_out
    y_prompt, conf_p = _out_call(
        functools.partial(_out_prompt_kernel, tm_out, tiles_per_seq, ff_chunk), "out_prompt", xp,
        [p_prompt[0].reshape(bp * seq, -1)], [o_raw, z, glu], (w_dw[0],) + norms + tail, tm_out, 1,
        jax.ShapeDtypeStruct((bp, n_dw - 1, cc), F32),
        pl.BlockSpec((1, n_dw - 1, cc), lambda i: (jnp.minimum(i, nt_out - 1) // tiles_per_seq, 0, 0)),
        [pltpu.VMEM((cpad + tm_out + SUBLANES, cc), F32), pltpu.VMEM((tm_out, dn + cc), BF16),
         pltpu.VMEM((GROUP_W, GROUP_W), BF16)])

    xs = jnp.swapaxes(x_sample, 0, 1).reshape(n_tok * bs, d)
    qkv, z, ba, glu = _inproj(xs, row(g_mix[0]), wt, gp, n_heads, dn, cc, tm=n_tok * bs)
    tmaj = lambda a: a.reshape(n_tok, bs, -1)
    o_raw, ssm_s, qkv_s = _seq_sample(jnp.swapaxes(state_qkv_conv[0], 0, 1), tmaj(qkv), tmaj(ba), state_ssm[0],
                                      w_short[0])
    ps = jnp.swapaxes(p_sample[0], 0, 1).reshape(n_tok * bs, -1)
    hc = jnp.swapaxes(state_conf_conv[0], 0, 1)
    ys, conf_s = _out_call(
        functools.partial(_out_sample_kernel, ff_chunk), "out_sample", xs,
        [ps], [o_raw, tmaj(z), tmaj(glu)], (w_dw[0],) + norms + (hc,) + tail, n_tok * bs, 0,
        jax.ShapeDtypeStruct(hc.shape, F32), pl.BlockSpec(hc.shape, lambda i: (0, 0, 0)), [])
    y_sample = jnp.swapaxes(ys.reshape(n_tok, bs, d), 0, 1)

    return (y_prompt.reshape(bp, seq, d), y_sample, ssm_p[None], qkv_p[None], conf_p[None], ssm_s[None],
            jnp.swapaxes(qkv_s, 0, 1)[None], jnp.swapaxes(conf_s, 0, 1)[None])
```

```python
import functools

import jax
import jax.numpy as jnp
from jax import lax
from jax.experimental import pallas as pl
from jax.experimental.pallas import tpu as pltpu

F32 = jnp.float32
BF16 = jnp.bfloat16
EPS = 1e-6
HEAD_DIM = 64
CHUNK = 64
GROUP_W = 256
HEADS_PER_GROUP = GROUP_W // HEAD_DIM
SUBLANES = 8
VMEM_LIMIT = 56 * 1024 * 1024


def _dot(a, b):
    return jnp.dot(a.astype(BF16), b.astype(BF16), preferred_element_type=F32)


def _dot_nt(a, b):
    return lax.dot_general(a.astype(BF16), b.astype(BF16), (((1,), (1,)), ((), ())), preferred_element_type=F32)


def _sigmoid(x):
    return 1.0 / (1.0 + jnp.exp(-x))


def _silu(x):
    return x * _sigmoid(x)


def _softplus(x):
    return jnp.maximum(x, 0.0) + jnp.log1p(jnp.exp(-jnp.abs(x)))


def _rms(x, gain):
    return x * lax.rsqrt(jnp.mean(x * x, axis=-1, keepdims=True) + EPS) * gain


def _split3(x):
    hi = x.astype(BF16)
    r = x - hi.astype(F32)
    mid = r.astype(BF16)
    lo = (r - mid.astype(F32)).astype(BF16)
    return hi, mid, lo


def _dot_exact(a01, x):
    hi, mid, lo = _split3(x)
    d = lambda p: jnp.dot(a01, p, preferred_element_type=F32)
    return d(hi) + (d(mid) + d(lo))


def _block_ones(n):
    shift = HEAD_DIM.bit_length() - 1
    r = lax.broadcasted_iota(jnp.int32, (n, n), 0) >> shift
    c = lax.broadcasted_iota(jnp.int32, (n, n), 1) >> shift
    return (r == c).astype(BF16)


def _group_sum(x, ones_bd):
    outs = [jnp.dot(x[:, GROUP_W * b:GROUP_W * (b + 1)].astype(BF16), ones_bd, preferred_element_type=F32)
            for b in range(x.shape[1] // GROUP_W)]
    return jnp.concatenate(outs, axis=1)


def _expand_heads(cols, off, n_heads):
    rows = cols.shape[0]
    lo_half = lax.broadcasted_iota(jnp.int32, (rows, 128), 1) < HEAD_DIM
    blocks = []
    for b in range(n_heads // 2):
        c0 = cols[:, off + 2 * b:off + 2 * b + 1]
        c1 = cols[:, off + 2 * b + 1:off + 2 * b + 2]
        blocks.append(jnp.where(lo_half, c0, c1))
    return jnp.concatenate(blocks, axis=1)


class _Masks:
    SHAPES = dict(strict=((CHUNK, GROUP_W), F32), causal=((CHUNK, GROUP_W), F32), eye_cat=((CHUNK, GROUP_W), F32),
                  bd=((GROUP_W, GROUP_W), BF16), bd_f32=((GROUP_W, GROUP_W), F32), tri=((CHUNK, CHUNK), BF16),
                  ones_cc=((CHUNK, CHUNK), BF16))

    @classmethod
    def scratch(cls):
        return [pltpu.VMEM(shape, dtype) for shape, dtype in cls.SHAPES.values()]

    def __init__(self, refs, pool_ref=None):
        self._refs = dict(zip(self.SHAPES, refs))
        self.pool = _BlockDiagPool(pool_ref)

    def __getattr__(self, name):
        return self._refs[name][...]

    def fill(self, seqs):
        c = CHUNK
        r = self._refs
        ri = lax.broadcasted_iota(jnp.int32, (c, GROUP_W), 0)
        lj = lax.broadcasted_iota(jnp.int32, (c, GROUP_W), 1) & (HEAD_DIM - 1)
        same = (ri & (seqs - 1)) == (lj & (seqs - 1))
        r["strict"][...] = (same & (ri > lj)).astype(F32)
        r["causal"][...] = (same & (ri >= lj)).astype(F32)
        r["eye_cat"][...] = (ri == lj).astype(F32)
        r["bd"][...] = _block_ones(GROUP_W)
        r["bd_f32"][...] = _block_ones(GROUP_W).astype(F32)
        r3 = lax.broadcasted_iota(jnp.int32, (c, c), 0)
        c3 = lax.broadcasted_iota(jnp.int32, (c, c), 1)
        r["tri"][...] = (((r3 & (seqs - 1)) == (c3 & (seqs - 1))) & (r3 >= c3)).astype(BF16)
        r["ones_cc"][...] = jnp.ones((c, c), BF16)


class _BlockDiagPool:
    def __init__(self, ref):
        self.ref = ref
        self.used = 0

    def zero(self):
        self.ref[...] = jnp.zeros(self.ref.shape, self.ref.dtype)

    def expand(self, y):
        n = self.used
        self.used += 1
        yb = y.astype(BF16)
        for h in range(HEADS_PER_GROUP):
            blk = slice(HEAD_DIM * h, HEAD_DIM * (h + 1))
            self.ref[n, blk, blk] = yb[:, blk]
        return self.ref[n]


def _block_diag(y, m):
    return m.pool.expand(y)


def _interleave(gens, weights):
    live = dict(zip(gens, weights))
    while live:
        for g, w in tuple(live.items()):
            for _ in range(w):
                try:
                    next(g)
                except StopIteration:
                    live.pop(g, None)
                    break


def _zero_from(v, width):
    bits = lax.bitcast_convert_type(v[0:1], jnp.int32)
    half = jnp.full(bits.shape, 16, jnp.int32)
    z = lax.shift_right_logical(lax.shift_right_logical(bits, half), half).astype(F32)
    return jnp.concatenate([z] * (width // z.shape[1]), axis=1)


def _drain(gen):
    for _ in gen:
        pass


def _prep_stage(out, acc, ba, g_last_of, dn, n_heads, m):
    qkv = _silu(acc)
    yield
    q, k, v = qkv[:, :dn], qkv[:, dn:2 * dn], qkv[:, 2 * dn:]
    qs = q * (lax.rsqrt(_group_sum(q * q, m.bd) + EPS) * (HEAD_DIM ** -0.5))
    k = k * lax.rsqrt(_group_sum(k * k, m.bd) + EPS)
    yield
    gcs = _dot_exact(m.tri, ba)
    bexp = _expand_heads(ba, 0, n_heads)
    g_row = _expand_heads(gcs, n_heads, n_heads)
    g_last = g_last_of(g_row)
    yield
    egc = jnp.exp(g_row)
    kb = k * bexp
    vb = v * bexp
    kbd = kb * egc
    qd = qs * egc
    kd = k * jnp.exp(g_last - g_row)
    yield
    for grp in range(dn // GROUP_W):
        sl = slice(GROUP_W * grp, GROUP_W * (grp + 1))
        gr = g_row[:, sl]
        g_col = _dot_exact(m.ones_cc, gr * m.eye_cat)
        dec = jnp.exp(jnp.minimum(gr - g_col, 0.0)) * m.causal
        aq = _dot_nt(jnp.concatenate([kb[:, sl], qs[:, sl]], axis=0), _block_diag(k[:, sl], m))
        out.append(dict(a=aq[:CHUNK] * (dec * m.strict), qk=aq[CHUNK:] * dec,
                        rhs=jnp.concatenate([_block_diag(vb[:, sl], m), _block_diag(kbd[:, sl], m)], axis=1),
                        qd=qd[:, sl], kd=kd[:, sl], g_last=g_last[:, sl]))
        yield


def _solve_stage(probs, m, levels):
    pws = [-p["a"] for p in probs]
    ts = [m.eye_cat + pw for pw in pws]
    for lvl in range(levels):
        last = lvl == levels - 1
        ops = [_block_diag(pw, m) for pw in pws]
        r1 = [_dot(t if last else jnp.concatenate([pw, t], axis=0), op) for pw, t, op in zip(pws, ts, ops)]
        if not last:
            pws = [r[:CHUNK] for r in r1]
        ts = [t + _dot(r[-CHUNK:], op) for t, r, op in zip(ts, r1, ops)]
        yield
    for p, t in zip(probs, ts):
        uw = _dot(t, p["rhs"])
        p["u"], p["w"] = uw[:, :GROUP_W], uw[:, GROUP_W:]
    yield


def _inproj_kernel(n_heads, dn, cc, x_ref, gmix_ref, wt_ref, gp_ref, qkv_ref, z_ref, ba_ref, glu_ref):
    hb = _rms(x_ref[...], gmix_ref[...]).astype(BF16)
    d = lambda r0, r1: lax.dot_general(hb, wt_ref[r0:r1], (((1,), (1,)), ((), ())), preferred_element_type=F32)
    o_z = 3 * dn
    o_ba = o_z + dn
    o_ga = o_ba + 2 * n_heads
    o_gb = o_ga + cc
    qkv_ref[...] = d(0, o_z)
    z_ref[...] = _silu(d(o_z, o_ba))
    glu_ref[...] = d(o_ga, o_gb) * _sigmoid(d(o_gb, o_gb + cc))
    ba = d(o_ba, o_ga)
    col = lax.broadcasted_iota(jnp.int32, ba.shape, 1)
    g = -jnp.exp(gp_ref[0:1, :]) * _softplus(ba + gp_ref[1:2, :])
    ba_ref[...] = jnp.where(col < n_heads, _sigmoid(ba), g)


def _inproj(x, gmix, wt, gp, n_heads, dn, cc, tm):
    t, d = x.shape
    const = lambda shape: pl.BlockSpec(shape, lambda i: (0, 0), pipeline_mode=pl.Buffered(1))
    row = lambda w: pl.BlockSpec((tm, w), lambda i: (i, 0))
    return pl.pallas_call(
        functools.partial(_inproj_kernel, n_heads, dn, cc),
        grid=(t // tm,),
        in_specs=[row(d), const(gmix.shape), const(wt.shape), const(gp.shape)],
        out_specs=[row(3 * dn), row(dn), row(2 * n_heads), row(cc)],
        out_shape=[jax.ShapeDtypeStruct((t, 3 * dn), F32), jax.ShapeDtypeStruct((t, dn), F32),
                   jax.ShapeDtypeStruct((t, 2 * n_heads), F32), jax.ShapeDtypeStruct((t, cc), F32)],
        compiler_params=pltpu.CompilerParams(dimension_semantics=("arbitrary",), vmem_limit_bytes=VMEM_LIMIT),
        name="inproj",
    )(x, gmix, wt, gp)


def _seq_prompt_kernel(dn, n_short, qkv_ref, ba_ref, wshort_ref, o_ref, sout_ref, qh_ref, hist, sbd, pool_ref,
                       *mask_refs):
    step = pl.program_id(0)
    n_seq = qkv_ref.shape[0]
    xpad = hist.shape[1]
    n_heads = dn // HEAD_DIM
    n_groups = dn // GROUP_W
    m = _Masks(mask_refs, pool_ref)
    levels = (CHUNK - 1).bit_length() - 1

    @pl.when(step == 0)
    def _():
        m.fill(1)
        m.pool.zero()
        hist[...] = jnp.zeros(hist.shape, F32)
        sbd[...] = jnp.zeros(sbd.shape, F32)

    probs = [[] for _ in range(n_seq)]

    def prep(b):
        window = jnp.concatenate([hist[b], qkv_ref[b]], axis=0)
        even = odd = None
        shifted = window
        for dist in range(0, n_short, 2):
            if dist:
                shifted = pltpu.roll(shifted, 2, 0)
            term = shifted * wshort_ref[n_short - 1 - dist:n_short - dist]
            even = term if even is None else even + term
            if dist + 1 < n_short:
                term = shifted * wshort_ref[n_short - 2 - dist:n_short - 1 - dist]
                odd = term if odd is None else odd + term
        acc = even if odd is None else even + pltpu.roll(odd, 1, 0)
        yield
        yield from _prep_stage(probs[b], acc[xpad:], ba_ref[b], lambda g_row: g_row[CHUNK - 1:CHUNK], dn, n_heads, m)

    def recur(b):
        o_parts = []
        for grp, p in enumerate(probs[b]):
            s = sbd[n_groups * b + grp]
            wq = _dot(jnp.concatenate([p["w"], p["qd"]], axis=0), s)
            v_new = p["u"] - wq[:CHUNK]
            yield
            o_parts.append(wq[CHUNK:] + _dot(p["qk"], _block_diag(v_new, m)))
            sbd[n_groups * b + grp] = s * jnp.exp(p["g_last"]) + _dot(p["kd"].T, v_new) * m.bd_f32
            yield
        o_ref[b] = jnp.concatenate(o_parts, axis=1)

    for b in range(n_seq):
        _drain(prep(b))
    _drain(_solve_stage([p for pb in probs for p in pb], m, levels))
    _interleave([recur(b) for b in range(n_seq)], [1] * n_seq)

    for b in range(n_seq):
        tail = qkv_ref[b, CHUNK - xpad:CHUNK]
        hist[b] = tail
        qh_ref[b] = tail[xpad - (n_short - 1):]

    @pl.when(step == pl.num_programs(0) - 1)
    def _():
        for b in range(n_seq):
            for h in range(n_heads):
                grp, hl = divmod(h, HEADS_PER_GROUP)
                sout_ref[b, h] = sbd[n_groups * b + grp, HEAD_DIM * hl:HEAD_DIM * (hl + 1),
                                     HEAD_DIM * hl:HEAD_DIM * (hl + 1)]


def _bd_uses(n_chunks, n_groups, levels=(CHUNK - 1).bit_length() - 1):
    return n_chunks * n_groups * (1 + 2 + levels + 1)


def _seq_prompt(qkv, ba, wshort, batch):
    t, dn3 = qkv.shape
    dn = dn3 // 3
    seq = t // batch
    n_heads = dn // HEAD_DIM
    n_groups = dn // GROUP_W
    n_short = wshort.shape[0]
    xpad = -(-(n_short - 1) // SUBLANES) * SUBLANES
    chunk_of = lambda w: pl.BlockSpec((batch, CHUNK, w), lambda c: (0, c, 0))
    whole = lambda shape: pl.BlockSpec(shape, lambda c: (0,) * len(shape))
    o, sout, qh = pl.pallas_call(
        functools.partial(_seq_prompt_kernel, dn, n_short),
        grid=(seq // CHUNK,),
        in_specs=[chunk_of(dn3), chunk_of(ba.shape[1]), whole(wshort.shape)],
        out_specs=[chunk_of(dn), whole((batch, n_heads, HEAD_DIM, HEAD_DIM)), whole((batch, n_short - 1, dn3))],
        out_shape=[jax.ShapeDtypeStruct((batch, seq, dn), F32),
                   jax.ShapeDtypeStruct((batch, n_heads, HEAD_DIM, HEAD_DIM), F32),
                   jax.ShapeDtypeStruct((batch, n_short - 1, dn3), F32)],
        scratch_shapes=[pltpu.VMEM((batch, xpad, dn3), F32), pltpu.VMEM((batch * n_groups, GROUP_W, GROUP_W), F32),
                        pltpu.VMEM((_bd_uses(batch, n_groups), GROUP_W, GROUP_W), BF16)] + _Masks.scratch(),
        compiler_params=pltpu.CompilerParams(dimension_semantics=("arbitrary",), vmem_limit_bytes=VMEM_LIMIT),
        name="seq_prompt",
    )(qkv.reshape(batch, seq, dn3), ba.reshape(batch, seq, ba.shape[1]), wshort)
    return o.reshape(t, dn), sout, qh


def _seq_sample_kernel(dn, hq_ref, xq_ref, ba_ref, s0_ref, wshort_ref, o_ref, sout_ref, nq_ref, pool_ref, *mask_refs):
    n_tok, seqs = xq_ref.shape[0], xq_ref.shape[1]
    n_heads = dn // HEAD_DIM
    n_short = wshort_ref.shape[0]
    m = _Masks(mask_refs, pool_ref)
    levels = max((n_tok - 1).bit_length() - 1, 0)

    @pl.when(pl.program_id(0) == 0)
    def _():
        m.fill(seqs)
        m.pool.zero()

    full_q = [hq_ref[j] for j in range(n_short - 1)] + [xq_ref[t] for t in range(n_tok)]
    for j in range(n_short - 1):
        nq_ref[j] = full_q[n_tok + j]
    rows = []
    for t in range(n_tok):
        acc = full_q[t] * wshort_ref[0:1]
        for j in range(1, n_short):
            acc = acc + full_q[t + j] * wshort_ref[j:j + 1]
        rows.append(acc)
    ba = jnp.concatenate([ba_ref[t] for t in range(n_tok)], axis=0)
    probs = []
    g_last_of = lambda g_row: jnp.concatenate([g_row[(n_tok - 1) * seqs:]] * n_tok, axis=0)
    _drain(_prep_stage(probs, jnp.concatenate(rows, axis=0), ba, g_last_of, dn, n_heads, m))
    _drain(_solve_stage(probs, m, levels))

    row_seq = lax.broadcasted_iota(jnp.int32, (2 * CHUNK, GROUP_W), 0) & (seqs - 1)
    col_seq = lax.broadcasted_iota(jnp.int32, (GROUP_W, CHUNK), 1) & (seqs - 1)
    o_parts = []
    for grp, p in enumerate(probs):
        lhs = jnp.concatenate([p["w"], p["qd"]], axis=0).astype(BF16)
        s_bd = []
        wq = jnp.zeros((2 * CHUNK, GROUP_W), F32)
        for s in range(seqs):
            s_rows = s0_ref[s, HEADS_PER_GROUP * grp:HEADS_PER_GROUP * (grp + 1)].reshape(GROUP_W, HEAD_DIM)
            s_full = jnp.concatenate([s_rows] * HEADS_PER_GROUP, axis=1) * m.bd_f32
            s_bd.append(s_full)
            wq = jnp.where(row_seq == s, _dot(lhs, s_full), wq)
        v_new = p["u"] - wq[:CHUNK]
        o_parts.append(wq[CHUNK:] + _dot(p["qk"], _block_diag(v_new, m)))
        kd_t = p["kd"].T.astype(BF16)
        for s in range(seqs):
            upd = _dot(jnp.where(col_seq == s, kd_t, jnp.zeros((), BF16)), v_new)
            s_new = s_bd[s] * jnp.exp(p["g_last"][s:s + 1]) + upd * m.bd_f32
            for hl in range(HEADS_PER_GROUP):
                sout_ref[s, HEADS_PER_GROUP * grp + hl] = s_new[HEAD_DIM * hl:HEAD_DIM * (hl + 1),
                                                                 HEAD_DIM * hl:HEAD_DIM * (hl + 1)]
    o = jnp.concatenate(o_parts, axis=1)
    for t in range(n_tok):
        o_ref[t] = o[t * seqs:(t + 1) * seqs]


def _seq_sample(hq, xq, ba, s0, wshort):
    n_tok, nseq, dn3 = xq.shape
    dn = dn3 // 3
    seqs = CHUNK // n_tok
    n_heads = dn // HEAD_DIM
    tm = lambda a: pl.BlockSpec((a.shape[0], seqs, a.shape[2]), lambda i: (0, i, 0))
    st = pl.BlockSpec((seqs, n_heads, HEAD_DIM, HEAD_DIM), lambda i: (i, 0, 0, 0))
    o_shape = jax.ShapeDtypeStruct((n_tok, nseq, dn), F32)
    return pl.pallas_call(
        functools.partial(_seq_sample_kernel, dn),
        grid=(nseq // seqs,),
        in_specs=[tm(hq), tm(xq), tm(ba), st, pl.BlockSpec(wshort.shape, lambda i: (0, 0))],
        out_specs=[tm(o_shape), st, tm(hq)],
        out_shape=[o_shape, jax.ShapeDtypeStruct(s0.shape, F32), jax.ShapeDtypeStruct(hq.shape, F32)],
        scratch_shapes=[pltpu.VMEM((_bd_uses(1, dn // GROUP_W, max((n_tok - 1).bit_length() - 1, 0)), GROUP_W, GROUP_W), BF16)]
        + _Masks.scratch(),
        compiler_params=pltpu.CompilerParams(dimension_semantics=("arbitrary",), vmem_limit_bytes=VMEM_LIMIT),
        name="seq_sample",
    )(hq, xq, ba, s0, wshort)


def _o_norm(o_raw, z_silu, go_ref, ones_bd):
    ms = _group_sum(o_raw * o_raw, ones_bd) * (1.0 / HEAD_DIM)
    return o_raw * lax.rsqrt(ms + EPS) * go_ref[...] * z_silu


def _conv_norm(conv, bdw_ref, gcln_ref, bcln_ref, ones_bd):
    inv = 1.0 / HEAD_DIM
    cv = conv + bdw_ref[...]
    d = cv - _group_sum(cv, ones_bd) * inv
    return _silu(d * lax.rsqrt(_group_sum(d * d, ones_bd) * inv + EPS) * gcln_ref[...] + bcln_ref[...])


def _fold_rows(x, prev=None):
    acc = x[0:SUBLANES] if prev is None else x[0:SUBLANES] + prev
    for r in range(SUBLANES, x.shape[0], SUBLANES):
        acc = acc + x[r:r + SUBLANES]
    return acc


def _tail_stage(y_ref, x, ocv, p, wout_ref, gmlp_ref, wup_ref, wdown_ref, gple_ref, wgate_ref, wple_ref, gfin_ref,
                ff_chunk, anchor=lambda: 0.0):
    def dot_cols(a, w_ref, r0=0, r1=None, c0=0, c1=None):
        a = a.astype(BF16)
        r1 = w_ref.shape[0] if r1 is None else r1
        c1 = w_ref.shape[1] if c1 is None else c1
        cols = []
        for c in range(c0, c1, GROUP_W):
            cols.append(jnp.dot(a, w_ref[r0:r1, c:c + GROUP_W], preferred_element_type=F32))
            yield
        return jnp.concatenate(cols, axis=1)

    x = x + (yield from dot_cols(ocv, wout_ref))
    hb = _rms(x, gmlp_ref[...] + anchor()).astype(BF16)
    mlp = None
    for f in range(wup_ref.shape[1] // ff_chunk):
        up = jnp.maximum((yield from dot_cols(hb, wup_ref, c0=f * ff_chunk, c1=(f + 1) * ff_chunk)), anchor())
        down = yield from dot_cols(up * up, wdown_ref, r0=f * ff_chunk, r1=(f + 1) * ff_chunk)
        mlp = down if mlp is None else mlp + down
    x = x + mlp
    gate = _sigmoid((yield from dot_cols(_rms(x, gple_ref[...] + anchor()), wgate_ref)))
    x = x + (yield from dot_cols(p, wple_ref)) * gate
    y_ref[...] = _rms(x, gfin_ref[...] + anchor())


def _mix_stage(progress, ocv_buf, nc_ref, o_ref, z_ref, glu_ref, wdw_ref, norm_refs, ones_ref, cbuf, tm, block):
    n_dw = wdw_ref.shape[0]
    cpad = cbuf.shape[0] - tm - SUBLANES
    first = cpad - (n_dw - 1)
    go_ref, bdw_ref, gcln_ref, bcln_ref = norm_refs
    cbuf[cpad:cpad + tm] = glu_ref[...]
    o = _o_norm(o_ref[...], z_ref[...], go_ref, ones_ref[...])
    progress.append(_fold_rows(o))
    yield
    blocks = []
    for r0 in range(0, tm, block):
        conv = None
        for s in range(SUBLANES):
            part = None
            for a in range((first + n_dw - 1) // SUBLANES + 1):
                j = SUBLANES * a + s - first
                if 0 <= j < n_dw:
                    lo = SUBLANES * a + r0
                    term = cbuf[lo:lo + block + SUBLANES] * wdw_ref[j:j + 1]
                    part = term if part is None else part + term
            if part is not None:
                conv = part[s:s + block] if conv is None else conv + part[s:s + block]
            yield
        blocks.append(conv)
        progress.append(_fold_rows(conv, progress[-1]))
    cbuf[0:cpad] = cbuf[tm:tm + cpad]
    nc_ref[0] = cbuf[cpad - (n_dw - 1):cpad]
    cv = _conv_norm(jnp.concatenate(blocks, axis=0), bdw_ref, gcln_ref, bcln_ref, ones_ref[...])
    progress.append(_fold_rows(cv, progress[-1]))
    ocv_buf[...] = jnp.concatenate([o, cv], axis=1).astype(BF16)


def _out_prompt_kernel(tm, tiles_per_seq, ff_chunk, x_ref, p_ref, o_ref, z_ref, glu_ref, wdw_ref, go_ref, bdw_ref,
                       gcln_ref, bcln_ref, *rest):
    *tail_refs, y_ref, nc_ref, cbuf, ocv_buf, ones_ref = rest
    step = pl.program_id(0)
    cpad = cbuf.shape[0] - tm - SUBLANES

    @pl.when(step == 0)
    def _():
        ones_ref[...] = _block_ones(GROUP_W)
        cbuf[cpad + tm:] = jnp.zeros((SUBLANES, cbuf.shape[1]), F32)

    @pl.when(step % tiles_per_seq == 0)
    def _():
        cbuf[0:cpad] = jnp.zeros((cpad, cbuf.shape[1]), F32)

    mix = lambda progress: _mix_stage(progress, ocv_buf, nc_ref, o_ref, z_ref, glu_ref, wdw_ref,
                                      (go_ref, bdw_ref, gcln_ref, bcln_ref), ones_ref, cbuf, tm, CHUNK)

    @pl.when(step == 0)
    def _():
        _drain(mix([]))

    @pl.when(step > 0)
    def _():
        progress = []
        anchor = lambda: _zero_from(progress[-1], x_ref.shape[1]) if progress else 0.0
        _interleave([_tail_stage(y_ref, x_ref[...], ocv_buf[...], p_ref[...], *tail_refs, ff_chunk, anchor),
                     mix(progress)], [2, 4])


def _out_sample_kernel(ff_chunk, x_ref, p_ref, o_ref, z_ref, glu_ref, wdw_ref, go_ref, bdw_ref, gcln_ref, bcln_ref,
                       hc_ref, *rest):
    *tail_refs, y_ref, nc_ref = rest
    n_tok = glu_ref.shape[0]
    n_dw = wdw_ref.shape[0]
    full_c = [hc_ref[j] for j in range(n_dw - 1)] + [glu_ref[t] for t in range(n_tok)]
    for j in range(n_dw - 1):
        nc_ref[j] = full_c[n_tok + j]
    rows = []
    for t in range(n_tok):
        acc = full_c[t] * wdw_ref[0:1]
        for j in range(1, n_dw):
            acc = acc + full_c[t + j] * wdw_ref[j:j + 1]
        rows.append(acc)
    cat = lambda ref: jnp.concatenate([ref[t] for t in range(n_tok)], axis=0)
    ones_bd = _block_ones(GROUP_W)
    ocv = jnp.concatenate([_o_norm(cat(o_ref), cat(z_ref), go_ref, ones_bd),
                           _conv_norm(jnp.concatenate(rows, axis=0), bdw_ref, gcln_ref, bcln_ref, ones_bd)],
                          axis=1).astype(BF16)
    _drain(_tail_stage(y_ref, x_ref[...], ocv, p_ref[...], *tail_refs, ff_chunk))


def _out_call(kernel_fn, name, x, behind, ahead, consts, tm, lag, extra_out_shape, extra_out_spec, scratch):
    t, d = x.shape
    nt = t // tm
    late = lambda i: jnp.maximum(i - lag, 0)
    early = lambda i: jnp.minimum(i, nt - 1)
    whole = lambda a, **kw: pl.BlockSpec(a.shape, lambda i: (0,) * a.ndim, **kw)
    tiled = lambda a, tile_of: (pl.BlockSpec((tm, a.shape[1]), lambda i: (tile_of(i), 0)) if a.ndim == 2
                                else whole(a))
    return pl.pallas_call(
        kernel_fn,
        grid=(nt + lag,),
        in_specs=[tiled(a, late) for a in (x, *behind)] + [tiled(a, early) for a in ahead]
        + [whole(a, pipeline_mode=pl.Buffered(1)) for a in consts],
        out_specs=[tiled(x, late), extra_out_spec],
        out_shape=[jax.ShapeDtypeStruct((t, d), F32), extra_out_shape],
        scratch_shapes=scratch,
        compiler_params=pltpu.CompilerParams(dimension_semantics=("arbitrary",), vmem_limit_bytes=VMEM_LIMIT),
        name=name,
    )(x, *behind, *ahead, *consts)


def kernel(x_prompt, x_sample, state_ssm, state_qkv_conv, state_conf_conv, p_prompt, p_sample, g_mix, w_in, w_short,
           a_log, dt_bias, g_o, w_dw, b_dw, g_cln, b_cln, w_out, g_mlp, w_up, w_down, w_ple, g_ple, w_ple_gate,
           g_final):
    depth = w_in.shape[0]
    assert depth == 1, "single-layer trunk"
    bp, seq, d = x_prompt.shape
    bs, n_tok, _ = x_sample.shape
    n_heads = a_log.shape[1]
    dn = n_heads * HEAD_DIM
    n_dw, cc = w_dw.shape[1], w_dw.shape[2]
    assert dn % GROUP_W == 0 and cc % GROUP_W == 0 and CHUNK % n_tok == 0
    tm_in, tm_out, ff_chunk = 1024, 512, 1024
    assert seq % tm_out == 0 and seq % CHUNK == 0 and (bp * seq) % tm_in == 0

    wt = jnp.swapaxes(w_in[0], 0, 1).astype(BF16)
    gp = jnp.zeros((2, 2 * n_heads), F32).at[0, n_heads:].set(a_log[0]).at[1, n_heads:].set(dt_bias[0])
    row = lambda a: a.reshape(1, -1)
    norms = (row(jnp.tile(g_o[0], n_heads)), row(b_dw[0]), row(g_cln[0]), row(b_cln[0]))
    tail = (w_out[0].astype(BF16), row(g_mlp[0]), w_up[0].astype(BF16), w_down[0].astype(BF16), row(g_ple[0]),
            w_ple_gate[0].astype(BF16), w_ple[0].astype(BF16), row(g_final))

    xp = x_prompt.reshape(bp * seq, d)
    qkv, z, ba, glu = _inproj(xp, row(g_mix[0]), wt, gp, n_heads, dn, cc, tm=tm_in)
    o_raw, ssm_p, qkv_p = _seq_prompt(qkv, ba, w_short[0], batch=bp)
    cpad = -(-(n_dw - 1) // SUBLANES) * SUBLANES
    nt_out, tiles_per_seq = bp * seq // tm_out, seq // tm_out
    y_prompt, conf_p = _out_call(
        functools.partial(_out_prompt_kernel, tm_out, tiles_per_seq, ff_chunk), "out_prompt", xp,
        [p_prompt[0].reshape(bp * seq, -1)], [o_raw, z, glu], (w_dw[0],) + norms + tail, tm_out, 1,
        jax.ShapeDtypeStruct((bp, n_dw - 1, cc), F32),
        pl.BlockSpec((1, n_dw - 1, cc), lambda i: (jnp.minimum(i, nt_out - 1) // tiles_per_seq, 0, 0)),
        [pltpu.VMEM((cpad + tm_out + SUBLANES, cc), F32), pltpu.VMEM((tm_out, dn + cc), BF16),
         pltpu.VMEM((GROUP_W, GROUP_W), BF16)])

    xs = jnp.swapaxes(x_sample, 0, 1).reshape(n_tok * bs, d)
    qkv, z, ba, glu = _inproj(xs, row(g_mix[0]), wt, gp, n_heads, dn, cc, tm=n_tok * bs)
    tmaj = lambda a: a.reshape(n_tok, bs, -1)
    o_raw, ssm_s, qkv_s = _seq_sample(jnp.swapaxes(state_qkv_conv[0], 0, 1), tmaj(qkv), tmaj(ba), state_ssm[0],
                                      w_short[0])
    ps = jnp.swapaxes(p_sample[0], 0, 1).reshape(n_tok * bs, -1)
    hc = jnp.swapaxes(state_conf_conv[0], 0, 1)
    ys, conf_s = _out_call(
        functools.partial(_out_sample_kernel, ff_chunk), "out_sample", xs,
        [ps], [o_raw, tmaj(z), tmaj(glu)], (w_dw[0],) + norms + (hc,) + tail, n_tok * bs, 0,
        jax.ShapeDtypeStruct(hc.shape, F32), pl.BlockSpec(hc.shape, lambda i: (0, 0, 0)), [])
    y_sample = jnp.swapaxes(ys.reshape(n_tok, bs, d), 0, 1)

    return (y_prompt.reshape(bp, seq, d), y_sample, ssm_p[None], qkv_p[None], conf_p[None], ssm_s[None],
            jnp.swapaxes(qkv_s, 0, 1)[None], jnp.swapaxes(conf_s, 0, 1)[None])
```
